```python
import math
import jax, jax.numpy as jnp
from jax import lax
import numpy as np

D_MODEL = 2048
BATCH = 1
SEQ = 8192
DEPTH = 4

N_MIXERS = 2
N_ATTN_LAYERS = (DEPTH + 1) // 2
N_LRU_LAYERS = DEPTH // 2

HEAD_DIM = 64
N_HEADS = D_MODEL // HEAD_DIM
N_KV_HEADS = N_HEADS // 8
GROUP = N_HEADS // N_KV_HEADS
WINDOW = 128
BLOCK = 128
Q_W = N_HEADS * HEAD_DIM
KV_W = N_KV_HEADS * HEAD_DIM
ATTN_IN_W = Q_W + 2 * KV_W + Q_W

LRU_W = D_MODEL
LRU_BLOCK_W = 256
N_LRU_BLOCKS = LRU_W // LRU_BLOCK_W
CONV_W = 4
C_RG = 8.0
LRU_IN_W = 2 * LRU_W

NORM_EPS = 1e-6
MASK_VALUE = -1e30

kernel_name = "hybrid_swa_sink_rglru_interleaved"


def rms_norm(x, g):
    x32 = x.astype(jnp.float32)
    y = x32 * lax.rsqrt(jnp.mean(x32 * x32, axis=-1, keepdims=True) + NORM_EPS)
    return (y * g.astype(jnp.float32)).astype(x.dtype)


def sliding_window_attention(h, w_in, w_out, sinks):
    B, S, _ = h.shape
    nb = S // BLOCK
    proj = h @ w_in
    q, k, v, gate = jnp.split(proj, [Q_W, Q_W + KV_W, Q_W + 2 * KV_W], axis=-1)
    q = q.reshape(B, nb, BLOCK, N_KV_HEADS, GROUP, HEAD_DIM)
    k = k.reshape(B, nb, BLOCK, N_KV_HEADS, HEAD_DIM)
    v = v.reshape(B, nb, BLOCK, N_KV_HEADS, HEAD_DIM)
    pad = ((0, 0), (1, 0), (0, 0), (0, 0), (0, 0))
    k_band = jnp.concatenate([jnp.pad(k, pad)[:, :-1], k], axis=2)
    v_band = jnp.concatenate([jnp.pad(v, pad)[:, :-1], v], axis=2)

    scale = 1.0 / math.sqrt(HEAD_DIM)
    scores = jnp.einsum('bnqkgd,bnskd->bnkgqs', q.astype(jnp.float32),
                        k_band.astype(jnp.float32)) * scale
    q_pos = jnp.arange(BLOCK)[:, None] + BLOCK
    k_pos = jnp.arange(2 * BLOCK)[None, :]
    rel = q_pos - k_pos
    in_window = (rel >= 0) & (rel < WINDOW)
    prev_exists = (jnp.arange(nb)[:, None, None] > 0) | (k_pos[None] >= BLOCK)
    mask = (in_window[None] & prev_exists)[:, None, None]
    scores = jnp.where(mask, scores, MASK_VALUE)

    sink = jnp.broadcast_to(
        sinks.astype(jnp.float32).reshape(1, 1, N_KV_HEADS, GROUP, 1, 1),
        scores.shape[:-1] + (1,))
    probs = jax.nn.softmax(jnp.concatenate([scores, sink], axis=-1), axis=-1)[..., :-1]
    out = jnp.einsum('bnkgqs,bnskd->bnqkgd', probs.astype(v.dtype), v_band)
    out = out.reshape(B, S, Q_W)
    return (out * jax.nn.silu(gate)) @ w_out


def _linear_combine(left, right):
    a1, b1 = left
    a2, b2 = right
    return a1 * a2, a2 * b1 + b2


def rglru_block(h, w_in, conv_w, conv_b, w_a, b_a, w_x, b_x, lam, w_out):
    B, S, _ = h.shape
    proj = h @ w_in
    xb, gate = jnp.split(proj, [LRU_W], axis=-1)
    xp = jnp.pad(xb, ((0, 0), (CONV_W - 1, 0), (0, 0)))
    xc = conv_b + sum(conv_w[tap] * xp[:, tap:tap + S] for tap in range(CONV_W))

    xg = xc.reshape(B, S, N_LRU_BLOCKS, LRU_BLOCK_W)
    r = jax.nn.sigmoid(jnp.einsum('bshi,hij->bshj', xg, w_a) + b_a).reshape(B, S, LRU_W)
    i = jax.nn.sigmoid(jnp.einsum('bshi,hij->bshj', xg, w_x) + b_x).reshape(B, S, LRU_W)

    log_a = -C_RG * r.astype(jnp.float32) * jax.nn.softplus(-lam.astype(jnp.float32))
    a = jnp.exp(log_a)
    mult = jnp.sqrt(-jnp.expm1(2.0 * log_a))
    b = mult * (i * xc).astype(jnp.float32)
    _, hs = lax.associative_scan(_linear_combine, (a, b), axis=1)
    y = hs.astype(h.dtype) * jax.nn.silu(gate)
    return y @ w_out


def setup_inputs(seed: int = 0) -> dict:
    key = jax.random.key(seed)
    ks = jax.random.split(key, 16)
    f32 = jnp.float32
    nrm = lambda k, shape, fan_in: jax.random.normal(k, shape, f32) * (fan_in ** -0.5)
    x = jax.random.normal(ks[0], (BATCH, SEQ, D_MODEL), f32)
    norm_pre = 1.0 + 0.05 * jax.random.normal(ks[1], (DEPTH, D_MODEL), f32)
    norm_post = 1.0 + 0.05 * jax.random.normal(ks[2], (DEPTH, D_MODEL), f32)

    attn_w_in = nrm(ks[3], (N_ATTN_LAYERS, D_MODEL, ATTN_IN_W), D_MODEL)
    attn_w_out = nrm(ks[4], (N_ATTN_LAYERS, Q_W, D_MODEL), Q_W)
    attn_sinks = 0.5 * jax.random.normal(ks[5], (N_ATTN_LAYERS, N_HEADS), f32)

    lru_w_in = nrm(ks[6], (N_LRU_LAYERS, D_MODEL, LRU_IN_W), D_MODEL)
    lru_conv_w = nrm(ks[7], (N_LRU_LAYERS, CONV_W, LRU_W), CONV_W)
    lru_conv_b = 0.01 * jax.random.normal(ks[8], (N_LRU_LAYERS, LRU_W), f32)
    lru_w_a = nrm(ks[9], (N_LRU_LAYERS, N_LRU_BLOCKS, LRU_BLOCK_W, LRU_BLOCK_W), LRU_BLOCK_W)
    lru_b_a = 0.01 * jax.random.normal(ks[10], (N_LRU_LAYERS, N_LRU_BLOCKS, LRU_BLOCK_W), f32)
    lru_w_x = nrm(ks[11], (N_LRU_LAYERS, N_LRU_BLOCKS, LRU_BLOCK_W, LRU_BLOCK_W), LRU_BLOCK_W)
    lru_b_x = 0.01 * jax.random.normal(ks[12], (N_LRU_LAYERS, N_LRU_BLOCKS, LRU_BLOCK_W), f32)
    u = jax.random.uniform(ks[13], (N_LRU_LAYERS, LRU_W), f32, 0.9, 0.999)
    a0 = u ** (1.0 / C_RG)
    lru_lambda = jnp.log(a0) - jnp.log1p(-a0)
    lru_w_out = nrm(ks[14], (N_LRU_LAYERS, LRU_W, D_MODEL), LRU_W)
    return {
        "x": x, "norm_pre": norm_pre, "norm_post": norm_post,
        "attn_w_in": attn_w_in, "attn_w_out": attn_w_out, "attn_sinks": attn_sinks,
        "lru_w_in": lru_w_in, "lru_conv_w": lru_conv_w, "lru_conv_b": lru_conv_b,
        "lru_w_a": lru_w_a, "lru_b_a": lru_b_a, "lru_w_x": lru_w_x, "lru_b_x": lru_b_x,
        "lru_lambda": lru_lambda, "lru_w_out": lru_w_out,
    }


def reference(x, norm_pre, norm_post, attn_w_in, attn_w_out, attn_sinks,
              lru_w_in, lru_conv_w, lru_conv_b, lru_w_a, lru_b_a, lru_w_x, lru_b_x,
              lru_lambda, lru_w_out):
    h = x
    for layer in range(DEPTH):
        u = rms_norm(h, norm_pre[layer])
        j = layer // N_MIXERS
        if layer % N_MIXERS == 0:
            y = sliding_window_attention(u, attn_w_in[j], attn_w_out[j], attn_sinks[j])
        else:
            y = rglru_block(u, lru_w_in[j], lru_conv_w[j], lru_conv_b[j], lru_w_a[j],
                            lru_b_a[j], lru_w_x[j], lru_b_x[j], lru_lambda[j], lru_w_out[j])
        h = h + rms_norm(y, norm_post[layer])
    return h
```

```python
import functools

import jax
import jax.numpy as jnp
from jax import lax
from jax.experimental import pallas as pl
from jax.experimental.pallas import tpu as pltpu

D_MODEL = 2048
SEQ = 8192
HEAD_DIM = 64
N_HEADS = 32
N_KV_HEADS = 4
GROUP = N_HEADS // N_KV_HEADS
BLOCK = 128
KV_W = N_KV_HEADS * HEAD_DIM
ATTN_IN_W = 2 * D_MODEL + 2 * KV_W
LRU_BLOCK_W = 256
N_LRU_BLOCKS = D_MODEL // LRU_BLOCK_W
CONV_W = 4
C_RG = 8.0
NORM_EPS = 1e-6
MASK_VALUE = -1e30

LANES = 128
SUBLANES = 8
N_PAIRS = N_HEADS // 2
PAIRS_PER_KV = GROUP // 2

TM = 256
VMEM_LIMIT = 56 * 1024 * 1024

_F32 = jnp.float32
_BF16 = jnp.bfloat16


def _rms(x, g):
    ms = jnp.mean(x * x, axis=-1, keepdims=True)
    return x * lax.rsqrt(ms + NORM_EPS) * g


def _dot(a, b):
    return jnp.dot(a, b, preferred_element_type=_F32)


def _dot_nt(a, b):
    return lax.dot_general(a, b, (((1,), (1,)), ((), ())), preferred_element_type=_F32)


def _attn_kernel(sink_ref, h_ref, gpre_ref, gpost_ref, win_ref, wout_ref, out_ref,
                 q3_ref, sg3_ref, y3_ref, klo_ref, khi_ref, vlo_ref, vhi_ref):
    step = pl.program_id(0)
    nb = TM // BLOCK

    @pl.when(step == 0)
    def _():
        z = jnp.zeros((N_KV_HEADS, BLOCK, LANES), _BF16)
        klo_ref[:, 0:BLOCK, :] = z
        khi_ref[:, 0:BLOCK, :] = z
        vlo_ref[:, 0:BLOCK, :] = z
        vhi_ref[:, 0:BLOCK, :] = z

    u = _rms(h_ref[...], gpre_ref[...]).astype(_BF16)

    q = (_dot(u, win_ref[:, 0:D_MODEL]) * (1.0 / 8.0)).astype(_BF16)
    for p in range(N_PAIRS):
        q3_ref[p] = q[:, p * LANES:(p + 1) * LANES]

    kv = _dot(u, win_ref[:, D_MODEL:D_MODEL + 2 * KV_W])
    lane_lo = lax.broadcasted_iota(jnp.int32, (TM, LANES), 1) < HEAD_DIM
    for t, (lo_ref, hi_ref) in enumerate(((klo_ref, khi_ref), (vlo_ref, vhi_ref))):
        for c in range(N_KV_HEADS // 2):
            col = t * KV_W + c * LANES
            orig = kv[:, col:col + LANES]
            sw = pltpu.roll(orig, HEAD_DIM, 1)
            lo_ref[2 * c, BLOCK:BLOCK + TM, :] = jnp.where(lane_lo, orig, 0.0).astype(_BF16)
            hi_ref[2 * c, BLOCK:BLOCK + TM, :] = jnp.where(lane_lo, 0.0, sw).astype(_BF16)
            lo_ref[2 * c + 1, BLOCK:BLOCK + TM, :] = jnp.where(lane_lo, sw, 0.0).astype(_BF16)
            hi_ref[2 * c + 1, BLOCK:BLOCK + TM, :] = jnp.where(lane_lo, 0.0, orig).astype(_BF16)

    gate = _dot(u, win_ref[:, D_MODEL + 2 * KV_W:ATTN_IN_W])
    sg = gate * jax.nn.sigmoid(gate)
    for p in range(N_PAIRS):
        sg3_ref[p] = sg[:, p * LANES:(p + 1) * LANES]

    qi = lax.broadcasted_iota(jnp.int32, (BLOCK, BLOCK), 0)
    kj = lax.broadcasted_iota(jnp.int32, (BLOCK, BLOCK), 1)
    tri = kj <= qi

    def softmax_band(s_full, sink, b):
        s_prev = s_full[:, 0:BLOCK]
        s_cur = s_full[:, BLOCK:2 * BLOCK]
        if b == 0:
            prev_exists = (kj + (step * TM - BLOCK)) >= 0
            s_prev = jnp.where(prev_exists, s_prev, MASK_VALUE)
        s = jnp.where(tri, s_cur, s_prev)
        m = jnp.maximum(jnp.max(s, axis=-1, keepdims=True), sink)
        e = jnp.exp(s - m)
        den = jnp.sum(e, axis=-1, keepdims=True) + jnp.exp(sink - m)
        pn = e * (1.0 / den)
        band = jnp.concatenate([jnp.where(tri, 0.0, pn), jnp.where(tri, pn, 0.0)], axis=1)
        return band.astype(_BF16)

    def pair_body(p, carry):
        kvh = p // PAIRS_PER_KV
        sink_e = sink_ref[2 * p]
        sink_o = sink_ref[2 * p + 1]
        for b in range(nb):
            rows = slice(b * BLOCK, (b + 1) * BLOCK)
            band = slice(b * BLOCK, (b + 2) * BLOCK)
            q2 = q3_ref[p, rows, :]
            s_e = _dot_nt(q2, klo_ref[kvh, band, :])
            s_o = _dot_nt(q2, khi_ref[kvh, band, :])
            p_e = softmax_band(s_e, sink_e, b)
            p_o = softmax_band(s_o, sink_o, b)
            o2 = _dot(p_e, vlo_ref[kvh, band, :]) + _dot(p_o, vhi_ref[kvh, band, :])
            y3_ref[p, rows, :] = (o2 * sg3_ref[p, rows, :]).astype(_BF16)
        return carry

    lax.fori_loop(0, N_PAIRS, pair_body, 0)

    for ref in (klo_ref, khi_ref, vlo_ref, vhi_ref):
        ref[:, 0:BLOCK, :] = ref[:, TM:TM + BLOCK, :]

    y = jnp.concatenate([y3_ref[p] for p in range(N_PAIRS)], axis=1)
    yo = _dot(y, wout_ref[...])
    out_ref[...] = h_ref[...] + _rms(yo, gpost_ref[...])


def _const_spec(shape):
    zeros = (0,) * len(shape)
    return pl.BlockSpec(shape, lambda i: zeros, pipeline_mode=pl.Buffered(1))


def _attn_layer(h, gpre, gpost, w_in, w_out, sinks):
    row_spec = pl.BlockSpec((TM, D_MODEL), lambda i: (i, 0))
    band_rows = BLOCK + TM
    return pl.pallas_call(
        _attn_kernel,
        out_shape=jax.ShapeDtypeStruct((SEQ, D_MODEL), _F32),
        grid=(SEQ // TM,),
        in_specs=[
            pl.BlockSpec(memory_space=pltpu.SMEM),
            row_spec,
            _const_spec((1, D_MODEL)),
            _const_spec((1, D_MODEL)),
            _const_spec((D_MODEL, ATTN_IN_W)),
            _const_spec((D_MODEL, D_MODEL)),
        ],
        out_specs=row_spec,
        scratch_shapes=[
            pltpu.VMEM((N_PAIRS, TM, LANES), _BF16),
            pltpu.VMEM((N_PAIRS, TM, LANES), _F32),
            pltpu.VMEM((N_PAIRS, TM, LANES), _BF16),
            pltpu.VMEM((N_KV_HEADS, band_rows, LANES), _BF16),
            pltpu.VMEM((N_KV_HEADS, band_rows, LANES), _BF16),
            pltpu.VMEM((N_KV_HEADS, band_rows, LANES), _BF16),
            pltpu.VMEM((N_KV_HEADS, band_rows, LANES), _BF16),
        ],
        compiler_params=pltpu.CompilerParams(
            dimension_semantics=("arbitrary",), vmem_limit_bytes=VMEM_LIMIT),
        name="attn_layer",
    )(sinks, h, gpre, gpost, w_in, w_out)


LRU_CHUNK = 512


def _lru_kernel(h_ref, gpre_ref, gpost_ref, win_ref, cw_ref, cb_ref, wa_ref, ba_ref,
                wx_ref, bx_ref, lam_ref, wout_ref, out_ref,
                xpad_ref, xc_ref, ra_ref, rx_ref, sg_ref, y_ref, sp_ref, hc_ref):
    step = pl.program_id(0)

    @pl.when(step == 0)
    def _():
        xpad_ref[0:SUBLANES, :] = jnp.zeros((SUBLANES, D_MODEL), _F32)
        hc_ref[...] = jnp.zeros((SUBLANES, D_MODEL), _F32)
        nl = -lam_ref[...]
        sp_ref[...] = jnp.maximum(nl, 0.0) + jnp.log1p(jnp.exp(-jnp.abs(nl)))

    u = _rms(h_ref[...], gpre_ref[...]).astype(_BF16)
    xpad_ref[SUBLANES:SUBLANES + TM, :] = _dot(u, win_ref[:, 0:D_MODEL])
    gate = _dot(u, win_ref[:, D_MODEL:2 * D_MODEL])
    sg_ref[...] = gate * jax.nn.sigmoid(gate)

    for blk in range(N_LRU_BLOCKS):
        cs = slice(blk * LRU_BLOCK_W, (blk + 1) * LRU_BLOCK_W)
        xc = cb_ref[0:1, cs]
        for tap in range(CONV_W):
            off = SUBLANES - (CONV_W - 1) + tap
            xc = xc + cw_ref[tap:tap + 1, cs] * xpad_ref[off:off + TM, cs]
        xc_ref[:, cs] = xc
        xcb = xc.astype(_BF16)
        ra_ref[:, cs] = _dot(xcb, wa_ref[blk])
        rx_ref[:, cs] = _dot(xcb, wx_ref[blk])

    xpad_ref[0:SUBLANES, :] = xpad_ref[TM:TM + SUBLANES, :]

    row = lax.broadcasted_iota(jnp.int32, (SUBLANES, LRU_CHUNK), 0)

    def group_body(gi, carry):
        r0 = pl.multiple_of(gi * SUBLANES, SUBLANES)
        rows = pl.ds(r0, SUBLANES)
        for c in range(D_MODEL // LRU_CHUNK):
            cs = slice(c * LRU_CHUNK, (c + 1) * LRU_CHUNK)
            xc = xc_ref[rows, cs]
            r = jax.nn.sigmoid(ra_ref[rows, cs] + ba_ref[:, cs])
            ig = jax.nn.sigmoid(rx_ref[rows, cs] + bx_ref[:, cs])
            log_a = (-C_RG * r) * sp_ref[:, cs]
            a = jnp.exp(log_a)
            mult = jnp.sqrt(-jnp.tanh(log_a) * (a * a + 1.0))
            b = mult * (ig * xc)
            for d in (1, 2, 4):
                keep = row >= d
                a_sh = jnp.where(keep, pltpu.roll(a, d, 0), 1.0)
                b_sh = jnp.where(keep, pltpu.roll(b, d, 0), 0.0)
                b = a * b_sh + b
                a = a * a_sh
            hs = a * hc_ref[:, cs] + b
            y_ref[rows, cs] = hs * sg_ref[rows, cs]
            hc_ref[:, cs] = jnp.broadcast_to(hs[SUBLANES - 1:SUBLANES, :], (SUBLANES, LRU_CHUNK))
        return carry

    lax.fori_loop(0, TM // SUBLANES, group_body, 0)

    yo = _dot(y_ref[...].astype(_BF16), wout_ref[...])
    out_ref[...] = h_ref[...] + _rms(yo, gpost_ref[...])


def _lru_layer(h, gpre, gpost, w_in, conv_w, conv_b, w_a, b_a, w_x, b_x, lam, w_out):
    row_spec = pl.BlockSpec((TM, D_MODEL), lambda i: (i, 0))
    vec8 = _const_spec((SUBLANES, D_MODEL))
    gate_w = _const_spec((N_LRU_BLOCKS, LRU_BLOCK_W, LRU_BLOCK_W))
    act = pltpu.VMEM((TM, D_MODEL), _F32)
    return pl.pallas_call(
        _lru_kernel,
        out_shape=jax.ShapeDtypeStruct((SEQ, D_MODEL), _F32),
        grid=(SEQ // TM,),
        in_specs=[
            row_spec,
            _const_spec((1, D_MODEL)),
            _const_spec((1, D_MODEL)),
            _const_spec((D_MODEL, 2 * D_MODEL)),
            _const_spec((CONV_W, D_MODEL)),
            _const_spec((1, D_MODEL)),
            gate_w, vec8, gate_w, vec8, vec8,
            _const_spec((D_MODEL, D_MODEL)),
        ],
        out_specs=row_spec,
        scratch_shapes=[
            pltpu.VMEM((SUBLANES + TM, D_MODEL), _F32),
            act, act, act, act, act,
            pltpu.VMEM((SUBLANES, D_MODEL), _F32),
            pltpu.VMEM((SUBLANES, D_MODEL), _F32),
        ],
        compiler_params=pltpu.CompilerParams(
            dimension_semantics=("arbitrary",), vmem_limit_bytes=VMEM_LIMIT),
        name="lru_layer",
    )(h, gpre, gpost, w_in, conv_w, conv_b, w_a, b_a, w_x, b_x, lam, w_out)


def _rep8(v):
    return jnp.broadcast_to(v.reshape(1, D_MODEL), (SUBLANES, D_MODEL))


def kernel(x, norm_pre, norm_post, attn_w_in, attn_w_out, attn_sinks, lru_w_in, lru_conv_w,
           lru_conv_b, lru_w_a, lru_b_a, lru_w_x, lru_b_x, lru_lambda, lru_w_out):
    h = x.reshape(SEQ, D_MODEL)
    for layer in range(norm_pre.shape[0]):
        j = layer // 2
        gpre = norm_pre[layer].reshape(1, D_MODEL)
        gpost = norm_post[layer].reshape(1, D_MODEL)
        if layer % 2 == 0:
            h = _attn_layer(h, gpre, gpost, attn_w_in[j].astype(_BF16),
                            attn_w_out[j].astype(_BF16), attn_sinks[j])
        else:
            h = _lru_layer(h, gpre, gpost, lru_w_in[j].astype(_BF16), lru_conv_w[j],
                           lru_conv_b[j].reshape(1, D_MODEL),
                           lru_w_a[j].astype(_BF16), _rep8(lru_b_a[j]),
                           lru_w_x[j].astype(_BF16), _rep8(lru_b_x[j]),
                           _rep8(lru_lambda[j]), lru_w_out[j].astype(_BF16))
    return h.reshape(x.shape)
```

```python
import jax
import jax.numpy as jnp
from jax import lax
from jax.experimental import pallas as pl
from jax.experimental.pallas import tpu as pltpu

D_MODEL = 2048
SEQ = 8192
HEAD_DIM = 64
N_HEADS = 32
N_KV_HEADS = 4
GROUP = N_HEADS // N_KV_HEADS
BLOCK = 128
KV_W = N_KV_HEADS * HEAD_DIM
LRU_BLOCK_W = 256
N_LRU_BLOCKS = D_MODEL // LRU_BLOCK_W
CONV_W = 4
C_RG = 8.0
NORM_EPS = 1e-6
MASK_VALUE = -1e30

LANES = 128
SUBLANES = 8
N_PAIRS = N_HEADS // 2
PAIRS_PER_KV = GROUP // 2

TM = 256
N_TILES = SEQ // TM
VMEM_LIMIT = 56 * 1024 * 1024

_F32 = jnp.float32
_BF16 = jnp.bfloat16


def _rms(x, g):
    ms = jnp.mean(x * x, axis=-1, keepdims=True)
    return x * lax.rsqrt(ms + NORM_EPS) * g


def _dot(a, b):
    return jnp.dot(a, b, preferred_element_type=_F32)


def _dot_nt(a, b):
    return lax.dot_general(a, b, (((1,), (1,)), ((), ())), preferred_element_type=_F32)


def _attn_kernel(sink_ref, h_ref, hprev_ref, gpre_ref, gpost_ref, wqg_ref, wkv_ref, wout_ref,
                 out_ref, u_ref, q3_ref, sg3_ref, y3_ref, klo_ref, khi_ref, vlo_ref, vhi_ref):
    step = pl.program_id(0)
    cur = step % 2
    prv = 1 - cur
    nb = TM // BLOCK
    kv_refs = (klo_ref, khi_ref, vlo_ref, vhi_ref)

    @pl.when(step == 0)
    def _():
        q3_ref[...] = jnp.zeros(q3_ref.shape, _BF16)
        sg3_ref[...] = jnp.zeros(sg3_ref.shape, _F32)
        for ref in kv_refs:
            ref[...] = jnp.zeros(ref.shape, _BF16)

    u_ref[...] = _rms(h_ref[...], gpre_ref[...]).astype(_BF16)

    def project_kv():
        kv = _dot(u_ref[...], wkv_ref[...])
        lane_lo = lax.broadcasted_iota(jnp.int32, (TM, LANES), 1) < HEAD_DIM
        for t, (lo_ref, hi_ref) in enumerate(((klo_ref, khi_ref), (vlo_ref, vhi_ref))):
            for c in range(N_KV_HEADS // 2):
                col = t * KV_W + c * LANES
                orig = kv[:, col:col + LANES]
                sw = pltpu.roll(orig, HEAD_DIM, 1)
                new = slice(BLOCK, BLOCK + TM)
                lo_ref[cur, 2 * c, new, :] = jnp.where(lane_lo, orig, 0.0).astype(_BF16)
                hi_ref[cur, 2 * c, new, :] = jnp.where(lane_lo, 0.0, sw).astype(_BF16)
                lo_ref[cur, 2 * c + 1, new, :] = jnp.where(lane_lo, sw, 0.0).astype(_BF16)
                hi_ref[cur, 2 * c + 1, new, :] = jnp.where(lane_lo, 0.0, orig).astype(_BF16)
        for ref in kv_refs:
            ref[cur, :, 0:BLOCK, :] = ref[prv, :, TM:TM + BLOCK, :]

    stack = PAIRS_PER_KV * BLOCK
    qi = lax.broadcasted_iota(jnp.int32, (stack, BLOCK), 0) & (BLOCK - 1)
    kj = lax.broadcasted_iota(jnp.int32, (stack, BLOCK), 1)
    tri = kj <= qi
    prev_exists = (kj + ((step - 1) * TM - BLOCK)) >= 0

    def softmax_band(s_full, sink_col, b):
        s_prev = s_full[:, 0:BLOCK]
        s_cur = s_full[:, BLOCK:2 * BLOCK]
        if b == 0:
            s_prev = jnp.where(prev_exists, s_prev, MASK_VALUE)
        s = jnp.where(tri, s_cur, s_prev)
        m = jnp.maximum(jnp.max(s, axis=-1, keepdims=True), sink_col)
        e = jnp.exp(s - m)
        den = jnp.sum(e, axis=-1, keepdims=True) + jnp.exp(sink_col - m)
        pn = e * (1.0 / den)
        band = jnp.concatenate([jnp.where(tri, 0.0, pn), jnp.where(tri, pn, 0.0)], axis=1)
        return band.astype(_BF16)

    def sink_column(kvh, odd):
        cols = [jnp.full((BLOCK, 1), sink_ref[2 * (PAIRS_PER_KV * kvh + i) + odd], _F32)
                for i in range(PAIRS_PER_KV)]
        return jnp.concatenate(cols, axis=0)

    for kvh in range(N_KV_HEADS):
        pairs = slice(PAIRS_PER_KV * kvh, PAIRS_PER_KV * (kvh + 1))
        scores = []
        for b in range(nb):
            rows = slice(b * BLOCK, (b + 1) * BLOCK)
            band = slice(b * BLOCK, (b + 2) * BLOCK)
            qs = q3_ref[prv, pairs, rows, :].reshape(stack, LANES)
            scores.append((_dot_nt(qs, klo_ref[prv, kvh, band, :]),
                           _dot_nt(qs, khi_ref[prv, kvh, band, :])))

        half = PAIRS_PER_KV * LANES
        qg = _dot(u_ref[...], wqg_ref[kvh])
        for i in range(PAIRS_PER_KV):
            p = PAIRS_PER_KV * kvh + i
            q3_ref[cur, p] = (qg[:, i * LANES:(i + 1) * LANES] * (1.0 / 8.0)).astype(_BF16)
            g = qg[:, half + i * LANES:half + (i + 1) * LANES]
            sg3_ref[cur, p] = g * jax.nn.sigmoid(g)

        sink_e = sink_column(kvh, 0)
        sink_o = sink_column(kvh, 1)
        for b in range(nb):
            rows = slice(b * BLOCK, (b + 1) * BLOCK)
            band = slice(b * BLOCK, (b + 2) * BLOCK)
            p_e = softmax_band(scores[b][0], sink_e, b)
            p_o = softmax_band(scores[b][1], sink_o, b)
            o2 = _dot(p_e, vlo_ref[prv, kvh, band, :]) + _dot(p_o, vhi_ref[prv, kvh, band, :])
            sg = sg3_ref[prv, pairs, rows, :].reshape(stack, LANES)
            y3_ref[pairs, rows, :] = (o2 * sg).astype(_BF16).reshape(PAIRS_PER_KV, BLOCK, LANES)

    y = jnp.concatenate([y3_ref[p] for p in range(N_PAIRS)], axis=1)
    yo = _dot(y, wout_ref[...])
    project_kv()
    out_ref[...] = hprev_ref[...] + _rms(yo, gpost_ref[...])


def _const_spec(shape):
    zeros = (0,) * len(shape)
    return pl.BlockSpec(shape, lambda i: zeros, pipeline_mode=pl.Buffered(1))


def _attn_layer(h, gpre, gpost, w_qg, w_kv, w_out, sinks):
    band_rows = BLOCK + TM
    lagged = lambda i: (jnp.maximum(i - 1, 0), 0)
    return pl.pallas_call(
        _attn_kernel,
        out_shape=jax.ShapeDtypeStruct((SEQ, D_MODEL), _F32),
        grid=(N_TILES + 1,),
        in_specs=[
            pl.BlockSpec(memory_space=pltpu.SMEM),
            pl.BlockSpec((TM, D_MODEL), lambda i: (jnp.minimum(i, N_TILES - 1), 0)),
            pl.BlockSpec((TM, D_MODEL), lagged),
            _const_spec((1, D_MODEL)),
            _const_spec((1, D_MODEL)),
            _const_spec((N_KV_HEADS, D_MODEL, 2 * GROUP * HEAD_DIM)),
            _const_spec((D_MODEL, 2 * KV_W)),
            _const_spec((D_MODEL, D_MODEL)),
        ],
        out_specs=pl.BlockSpec((TM, D_MODEL), lagged),
        scratch_shapes=[
            pltpu.VMEM((TM, D_MODEL), _BF16),
            pltpu.VMEM((2, N_PAIRS, TM, LANES), _BF16),
            pltpu.VMEM((2, N_PAIRS, TM, LANES), _F32),
            pltpu.VMEM((N_PAIRS, TM, LANES), _BF16),
            pltpu.VMEM((2, N_KV_HEADS, band_rows, LANES), _BF16),
            pltpu.VMEM((2, N_KV_HEADS, band_rows, LANES), _BF16),
            pltpu.VMEM((2, N_KV_HEADS, band_rows, LANES), _BF16),
            pltpu.VMEM((2, N_KV_HEADS, band_rows, LANES), _BF16),
        ],
        compiler_params=pltpu.CompilerParams(
            dimension_semantics=("arbitrary",), vmem_limit_bytes=VMEM_LIMIT),
        name="attn_layer",
    )(sinks, h, h, gpre, gpost, w_qg, w_kv, w_out)


def _attn_weights(w_in, w_out):
    per_kv = GROUP * HEAD_DIM
    w_q = w_in[:, 0:D_MODEL].reshape(D_MODEL, N_KV_HEADS, per_kv)
    w_g = w_in[:, D_MODEL + 2 * KV_W:].reshape(D_MODEL, N_KV_HEADS, per_kv)
    w_qg = jnp.concatenate([w_q, w_g], axis=2).transpose(1, 0, 2).astype(_BF16)
    w_kv = w_in[:, D_MODEL:D_MODEL + 2 * KV_W].astype(_BF16)
    return w_qg, w_kv, w_out.astype(_BF16)


LRU_CHUNK = 512


def _lru_kernel(h_ref, gpre_ref, gpost_ref, win_ref, cw_ref, cb_ref, wa_ref, ba_ref,
                wx_ref, bx_ref, lam_ref, wout_ref, out_ref,
                xpad_ref, xc_ref, ra_ref, rx_ref, sg_ref, y_ref, sp_ref, hc_ref):
    step = pl.program_id(0)

    @pl.when(step == 0)
    def _():
        xpad_ref[0:SUBLANES, :] = jnp.zeros((SUBLANES, D_MODEL), _F32)
        hc_ref[...] = jnp.zeros((SUBLANES, D_MODEL), _F32)
        nl = -lam_ref[...]
        sp_ref[...] = jnp.maximum(nl, 0.0) + jnp.log1p(jnp.exp(-jnp.abs(nl)))

    u = _rms(h_ref[...], gpre_ref[...]).astype(_BF16)
    xpad_ref[SUBLANES:SUBLANES + TM, :] = _dot(u, win_ref[:, 0:D_MODEL])
    gate = _dot(u, win_ref[:, D_MODEL:2 * D_MODEL])
    sg_ref[...] = gate * jax.nn.sigmoid(gate)

    for blk in range(N_LRU_BLOCKS):
        cs = slice(blk * LRU_BLOCK_W, (blk + 1) * LRU_BLOCK_W)
        xc = cb_ref[0:1, cs]
        for tap in range(CONV_W):
            off = SUBLANES - (CONV_W - 1) + tap
            xc = xc + cw_ref[tap:tap + 1, cs] * xpad_ref[off:off + TM, cs]
        xc_ref[:, cs] = xc
        xcb = xc.astype(_BF16)
        ra_ref[:, cs] = _dot(xcb, wa_ref[blk])
        rx_ref[:, cs] = _dot(xcb, wx_ref[blk])

    xpad_ref[0:SUBLANES, :] = xpad_ref[TM:TM + SUBLANES, :]

    row = lax.broadcasted_iota(jnp.int32, (SUBLANES, LRU_CHUNK), 0)

    def group_body(gi, carry):
        r0 = pl.multiple_of(gi * SUBLANES, SUBLANES)
        rows = pl.ds(r0, SUBLANES)
        for c in range(D_MODEL // LRU_CHUNK):
            cs = slice(c * LRU_CHUNK, (c + 1) * LRU_CHUNK)
            xc = xc_ref[rows, cs]
            r = jax.nn.sigmoid(ra_ref[rows, cs] + ba_ref[:, cs])
            ig = jax.nn.sigmoid(rx_ref[rows, cs] + bx_ref[:, cs])
            log_a = (-C_RG * r) * sp_ref[:, cs]
            a = jnp.exp(log_a)
            mult = jnp.sqrt(-jnp.tanh(log_a) * (a * a + 1.0))
            b = mult * (ig * xc)
            for d in (1, 2, 4):
                keep = row >= d
                a_sh = jnp.where(keep, pltpu.roll(a, d, 0), 1.0)
                b_sh = jnp.where(keep, pltpu.roll(b, d, 0), 0.0)
                b = a * b_sh + b
                a = a * a_sh
            hs = a * hc_ref[:, cs] + b
            y_ref[rows, cs] = hs * sg_ref[rows, cs]
            hc_ref[:, cs] = jnp.broadcast_to(hs[SUBLANES - 1:SUBLANES, :], (SUBLANES, LRU_CHUNK))
        return carry

    lax.fori_loop(0, TM // SUBLANES, group_body, 0)

    yo = _dot(y_ref[...].astype(_BF16), wout_ref[...])
    out_ref[...] = h_ref[...] + _rms(yo, gpost_ref[...])


def _lru_layer(h, gpre, gpost, w_in, conv_w, conv_b, w_a, b_a, w_x, b_x, lam, w_out):
    row_spec = pl.BlockSpec((TM, D_MODEL), lambda i: (i, 0))
    vec8 = _const_spec((SUBLANES, D_MODEL))
    gate_w = _const_spec((N_LRU_BLOCKS, LRU_BLOCK_W, LRU_BLOCK_W))
    act = pltpu.VMEM((TM, D_MODEL), _F32)
    return pl.pallas_call(
        _lru_kernel,
        out_shape=jax.ShapeDtypeStruct((SEQ, D_MODEL), _F32),
        grid=(N_TILES,),
        in_specs=[
            row_spec,
            _const_spec((1, D_MODEL)),
            _const_spec((1, D_MODEL)),
            _const_spec((D_MODEL, 2 * D_MODEL)),
            _const_spec((CONV_W, D_MODEL)),
            _const_spec((1, D_MODEL)),
            gate_w, vec8, gate_w, vec8, vec8,
            _const_spec((D_MODEL, D_MODEL)),
        ],
        out_specs=row_spec,
        scratch_shapes=[
            pltpu.VMEM((SUBLANES + TM, D_MODEL), _F32),
            act, act, act, act, act,
            pltpu.VMEM((SUBLANES, D_MODEL), _F32),
            pltpu.VMEM((SUBLANES, D_MODEL), _F32),
        ],
        compiler_params=pltpu.CompilerParams(
            dimension_semantics=("arbitrary",), vmem_limit_bytes=VMEM_LIMIT),
        name="lru_layer",
    )(h, gpre, gpost, w_in, conv_w, conv_b, w_a, b_a, w_x, b_x, lam, w_out)


def _rep8(v):
    return jnp.broadcast_to(v.reshape(1, D_MODEL), (SUBLANES, D_MODEL))


def kernel(x, norm_pre, norm_post, attn_w_in, attn_w_out, attn_sinks, lru_w_in, lru_conv_w,
           lru_conv_b, lru_w_a, lru_b_a, lru_w_x, lru_b_x, lru_lambda, lru_w_out):
    h = x.reshape(SEQ, D_MODEL)
    for layer in range(norm_pre.shape[0]):
        j = layer // 2
        gpre = norm_pre[layer].reshape(1, D_MODEL)
        gpost = norm_post[layer].reshape(1, D_MODEL)
        if layer % 2 == 0:
            h = _attn_layer(h, gpre, gpost, *_attn_weights(attn_w_in[j], attn_w_out[j]),
                            attn_sinks[j])
        else:
            h = _lru_layer(h, gpre, gpost, lru_w_in[j].astype(_BF16), lru_conv_w[j],
                           lru_conv_b[j].reshape(1, D_MODEL),
                           lru_w_a[j].astype(_BF16), _rep8(lru_b_a[j]),
                           lru_w_x[j].astype(_BF16), _rep8(lru_b_x[j]),
                           _rep8(lru_lambda[j]), lru_w_out[j].astype(_BF16))
    return h.reshape(x.shape)
```

```python
import functools

import jax
import jax.numpy as jnp
from jax import lax
from jax.experimental import pallas as pl
from jax.experimental.pallas import tpu as pltpu

D_MODEL = 2048
SEQ = 8192
HEAD_DIM = 64
N_HEADS = 32
N_KV_HEADS = 4
GROUP = N_HEADS // N_KV_HEADS
BLOCK = 128
KV_W = N_KV_HEADS * HEAD_DIM
GATE_COL = D_MODEL + 2 * KV_W
LRU_BLOCK_W = 256
N_LRU_BLOCKS = D_MODEL // LRU_BLOCK_W
CONV_W = 4
C_RG = 8.0
NORM_EPS = 1e-6
MASK_VALUE = -1e30

LANES = 128
SUBLANES = 8
N_PAIRS = N_HEADS // 2
PAIRS_PER_KV = GROUP // 2

TM = 256
N_TILES = SEQ // TM
VMEM_LIMIT = 56 * 1024 * 1024

_F32 = jnp.float32
_BF16 = jnp.bfloat16


def _rms(x, g):
    ms = jnp.mean(x * x, axis=-1, keepdims=True)
    return x * lax.rsqrt(ms + NORM_EPS) * g


def _dot(a, b):
    return jnp.dot(a, b, preferred_element_type=_F32)


def _dot_nt(a, b):
    return lax.dot_general(a, b, (((1,), (1,)), ((), ())), preferred_element_type=_F32)


def _attn_kernel(j, sink_ref, h_ref, hprev_ref, gpre_ref, gpost_ref, win_ref, wout_ref,
                 out_ref, u_ref, q3_ref, sg3_ref, y3_ref, klo_ref, khi_ref, vlo_ref, vhi_ref):
    step = pl.program_id(0)
    cur = step % 2
    prv = 1 - cur
    nb = TM // BLOCK
    kv_refs = (klo_ref, khi_ref, vlo_ref, vhi_ref)

    @pl.when(step == 0)
    def _():
        q3_ref[...] = jnp.zeros(q3_ref.shape, _BF16)
        sg3_ref[...] = jnp.zeros(sg3_ref.shape, _F32)
        for ref in kv_refs:
            ref[...] = jnp.zeros(ref.shape, _BF16)

    u_ref[...] = _rms(h_ref[...], gpre_ref[...]).astype(_BF16)

    def project_kv():
        kv = _dot(u_ref[...], win_ref[:, D_MODEL:GATE_COL])
        lane_lo = lax.broadcasted_iota(jnp.int32, (TM, LANES), 1) < HEAD_DIM
        for t, (lo_ref, hi_ref) in enumerate(((klo_ref, khi_ref), (vlo_ref, vhi_ref))):
            for c in range(N_KV_HEADS // 2):
                col = t * KV_W + c * LANES
                orig = kv[:, col:col + LANES]
                sw = pltpu.roll(orig, HEAD_DIM, 1)
                new = slice(BLOCK, BLOCK + TM)
                lo_ref[cur, 2 * c, new, :] = jnp.where(lane_lo, orig, 0.0).astype(_BF16)
                hi_ref[cur, 2 * c, new, :] = jnp.where(lane_lo, 0.0, sw).astype(_BF16)
                lo_ref[cur, 2 * c + 1, new, :] = jnp.where(lane_lo, sw, 0.0).astype(_BF16)
                hi_ref[cur, 2 * c + 1, new, :] = jnp.where(lane_lo, 0.0, orig).astype(_BF16)
        for ref in kv_refs:
            ref[cur, :, 0:BLOCK, :] = ref[prv, :, TM:TM + BLOCK, :]

    stack = PAIRS_PER_KV * BLOCK
    qi = lax.broadcasted_iota(jnp.int32, (stack, BLOCK), 0) & (BLOCK - 1)
    kj = lax.broadcasted_iota(jnp.int32, (stack, BLOCK), 1)
    tri = kj <= qi
    prev_exists = (kj + ((step - 1) * TM - BLOCK)) >= 0

    def softmax_band(s_full, sink_col, b):
        s_prev = s_full[:, 0:BLOCK]
        s_cur = s_full[:, BLOCK:2 * BLOCK]
        if b == 0:
            s_prev = jnp.where(prev_exists, s_prev, MASK_VALUE)
        s = jnp.where(tri, s_cur, s_prev)
        m = jnp.maximum(jnp.max(s, axis=-1, keepdims=True), sink_col)
        e = jnp.exp(s - m)
        den = jnp.sum(e, axis=-1, keepdims=True) + jnp.exp(sink_col - m)
        pn = e * (1.0 / den)
        band = jnp.concatenate([jnp.where(tri, 0.0, pn), jnp.where(tri, pn, 0.0)], axis=1)
        return band.astype(_BF16)

    def sink_column(kvh, odd):
        cols = [jnp.full((BLOCK, 1), sink_ref[j, 2 * (PAIRS_PER_KV * kvh + i) + odd], _F32)
                for i in range(PAIRS_PER_KV)]
        return jnp.concatenate(cols, axis=0)

    for kvh in range(N_KV_HEADS):
        pairs = slice(PAIRS_PER_KV * kvh, PAIRS_PER_KV * (kvh + 1))
        scores = []
        for b in range(nb):
            rows = slice(b * BLOCK, (b + 1) * BLOCK)
            band = slice(b * BLOCK, (b + 2) * BLOCK)
            qs = q3_ref[prv, pairs, rows, :].reshape(stack, LANES)
            scores.append((_dot_nt(qs, klo_ref[prv, kvh, band, :]),
                           _dot_nt(qs, khi_ref[prv, kvh, band, :])))

        per_kv = GROUP * HEAD_DIM
        q = _dot(u_ref[...], win_ref[:, kvh * per_kv:(kvh + 1) * per_kv])
        gate = _dot(u_ref[...], win_ref[:, GATE_COL + kvh * per_kv:GATE_COL + (kvh + 1) * per_kv])
        for i in range(PAIRS_PER_KV):
            p = PAIRS_PER_KV * kvh + i
            q3_ref[cur, p] = (q[:, i * LANES:(i + 1) * LANES] * (1.0 / 8.0)).astype(_BF16)
            g = gate[:, i * LANES:(i + 1) * LANES]
            sg3_ref[cur, p] = g * jax.nn.sigmoid(g)

        sink_e = sink_column(kvh, 0)
        sink_o = sink_column(kvh, 1)
        for b in range(nb):
            rows = slice(b * BLOCK, (b + 1) * BLOCK)
            band = slice(b * BLOCK, (b + 2) * BLOCK)
            p_e = softmax_band(scores[b][0], sink_e, b)
            p_o = softmax_band(scores[b][1], sink_o, b)
            o2 = _dot(p_e, vlo_ref[prv, kvh, band, :]) + _dot(p_o, vhi_ref[prv, kvh, band, :])
            sg = sg3_ref[prv, pairs, rows, :].reshape(stack, LANES)
            y3_ref[pairs, rows, :] = (o2 * sg).astype(_BF16).reshape(PAIRS_PER_KV, BLOCK, LANES)

    y = jnp.concatenate([y3_ref[p] for p in range(N_PAIRS)], axis=1)
    yo = _dot(y, wout_ref[...])
    project_kv()
    out_ref[...] = hprev_ref[...] + _rms(yo, gpost_ref[...])


def _layer_spec(stacked, j):
    zeros = (0,) * (stacked.ndim - 1)
    return pl.BlockSpec((None,) + stacked.shape[1:], lambda i: (j,) + zeros,
                        pipeline_mode=pl.Buffered(1))


def _attn_layer(h, layer, j, norm_pre, norm_post, w_in, w_out, sinks):
    band_rows = BLOCK + TM
    lagged = lambda i: (jnp.maximum(i - 1, 0), 0)
    return pl.pallas_call(
        functools.partial(_attn_kernel, j),
        out_shape=jax.ShapeDtypeStruct((SEQ, D_MODEL), _F32),
        grid=(N_TILES + 1,),
        in_specs=[
            pl.BlockSpec(memory_space=pltpu.SMEM),
            pl.BlockSpec((TM, D_MODEL), lambda i: (jnp.minimum(i, N_TILES - 1), 0)),
            pl.BlockSpec((TM, D_MODEL), lagged),
            _layer_spec(norm_pre, layer),
            _layer_spec(norm_post, layer),
            _layer_spec(w_in, j),
            _layer_spec(w_out, j),
        ],
        out_specs=pl.BlockSpec((TM, D_MODEL), lagged),
        scratch_shapes=[
            pltpu.VMEM((TM, D_MODEL), _BF16),
            pltpu.VMEM((2, N_PAIRS, TM, LANES), _BF16),
            pltpu.VMEM((2, N_PAIRS, TM, LANES), _F32),
            pltpu.VMEM((N_PAIRS, TM, LANES), _BF16),
            pltpu.VMEM((2, N_KV_HEADS, band_rows, LANES), _BF16),
            pltpu.VMEM((2, N_KV_HEADS, band_rows, LANES), _BF16),
            pltpu.VMEM((2, N_KV_HEADS, band_rows, LANES), _BF16),
            pltpu.VMEM((2, N_KV_HEADS, band_rows, LANES), _BF16),
        ],
        compiler_params=pltpu.CompilerParams(
            dimension_semantics=("arbitrary",), vmem_limit_bytes=VMEM_LIMIT),
        name="attn_layer",
    )(sinks, h, h, norm_pre, norm_post, w_in, w_out)


def _lru_kernel(h_ref, hprev_ref, gpre_ref, gpost_ref, win_ref, cw_ref, cb_ref, wa_ref, ba_ref,
                wx_ref, bx_ref, lam_ref, wout_ref, out_ref,
                u_ref, xpad_ref, y_ref, yo_ref, sp_ref, hc_ref):
    step = pl.program_id(0)
    cur = step % 2
    prv = 1 - cur
    groups = TM // SUBLANES

    @pl.when(step == 0)
    def _():
        xpad_ref[0:SUBLANES, :] = jnp.zeros((SUBLANES, D_MODEL), _F32)
        hc_ref[...] = jnp.zeros((SUBLANES, D_MODEL), _F32)
        y_ref[...] = jnp.zeros(y_ref.shape, _BF16)
        nl = -lam_ref[...]
        softplus = jnp.maximum(nl, 0.0) + jnp.log1p(jnp.exp(-jnp.abs(nl)))
        sp_ref[...] = jnp.broadcast_to(softplus, sp_ref.shape)

    u_ref[...] = _rms(h_ref[...], gpre_ref[...]).astype(_BF16)
    row = lax.broadcasted_iota(jnp.int32, (SUBLANES, LRU_BLOCK_W), 0)

    def block_cols(blk):
        return slice(blk * LRU_BLOCK_W, (blk + 1) * LRU_BLOCK_W)

    def project_x(blk):
        cs = block_cols(blk)
        xpad_ref[SUBLANES:SUBLANES + TM, cs] = _dot(u_ref[...], win_ref[:, cs])

    def out_project(blk):
        cs = block_cols(blk)
        yo_ref[:, cs] = _dot(y_ref[prv], wout_ref[:, cs])

    out_project(0)
    project_x(0)
    for blk in range(N_LRU_BLOCKS):
        cs = block_cols(blk)
        if blk + 1 < N_LRU_BLOCKS:
            project_x(blk + 1)
            out_project(blk + 1)
        gate = _dot(u_ref[...], win_ref[:, D_MODEL + blk * LRU_BLOCK_W:D_MODEL + (blk + 1) * LRU_BLOCK_W])

        xe = xpad_ref[:, cs]
        xc = cb_ref[0:1, cs] + cw_ref[CONV_W - 1:CONV_W, cs] * xe[SUBLANES:]
        for shift in range(1, CONV_W):
            tap = CONV_W - 1 - shift
            xc = xc + cw_ref[tap:tap + 1, cs] * pltpu.roll(xe, shift, 0)[SUBLANES:]
        xpad_ref[0:SUBLANES, cs] = xpad_ref[TM:TM + SUBLANES, cs]
        xcb = xc.astype(_BF16)
        r = jax.nn.sigmoid(_dot(xcb, wa_ref[blk]) + ba_ref[0:1, cs])
        ig = jax.nn.sigmoid(_dot(xcb, wx_ref[blk]) + bx_ref[0:1, cs])
        sg = gate * jax.nn.sigmoid(gate)

        nla = (C_RG * r) * sp_ref[0:1, cs]
        a = jnp.exp(-nla)
        z = jnp.tanh(nla) * (a * a + 1.0)
        mult = jnp.where(z > 0.0, z * lax.rsqrt(z), 0.0)
        b = mult * (ig * xc)

        carry = hc_ref[:, cs]
        ys = []
        for g in range(groups):
            rows = slice(g * SUBLANES, (g + 1) * SUBLANES)
            a8, b8 = a[rows], b[rows]
            for d in (1, 2, 4):
                keep = row >= d
                a_sh = jnp.where(keep, pltpu.roll(a8, d, 0), 1.0)
                b_sh = jnp.where(keep, pltpu.roll(b8, d, 0), 0.0)
                b8 = a8 * b_sh + b8
                a8 = a8 * a_sh
            hs = a8 * carry + b8
            ys.append(hs * sg[rows])
            carry = jnp.broadcast_to(hs[SUBLANES - 1:SUBLANES, :], (SUBLANES, LRU_BLOCK_W))
        hc_ref[:, cs] = carry
        y_ref[cur, :, cs] = jnp.concatenate(ys, axis=0).astype(_BF16)

    out_ref[...] = hprev_ref[...] + _rms(yo_ref[...], gpost_ref[...])


def _lru_layer(h, layer, j, norm_pre, norm_post, *params):
    lagged = lambda i: (jnp.maximum(i - 1, 0), 0)
    return pl.pallas_call(
        _lru_kernel,
        out_shape=jax.ShapeDtypeStruct((SEQ, D_MODEL), _F32),
        grid=(N_TILES + 1,),
        in_specs=[
            pl.BlockSpec((TM, D_MODEL), lambda i: (jnp.minimum(i, N_TILES - 1), 0)),
            pl.BlockSpec((TM, D_MODEL), lagged),
            _layer_spec(norm_pre, layer),
            _layer_spec(norm_post, layer),
        ] + [_layer_spec(p, j) for p in params],
        out_specs=pl.BlockSpec((TM, D_MODEL), lagged),
        scratch_shapes=[
            pltpu.VMEM((TM, D_MODEL), _BF16),
            pltpu.VMEM((SUBLANES + TM, D_MODEL), _F32),
            pltpu.VMEM((2, TM, D_MODEL), _BF16),
            pltpu.VMEM((TM, D_MODEL), _F32),
            pltpu.VMEM((SUBLANES, D_MODEL), _F32),
            pltpu.VMEM((SUBLANES, D_MODEL), _F32),
        ],
        compiler_params=pltpu.CompilerParams(
            dimension_semantics=("arbitrary",), vmem_limit_bytes=VMEM_LIMIT),
        name="lru_layer",
    )(h, h, norm_pre, norm_post, *params)


def kernel(x, norm_pre, norm_post, attn_w_in, attn_w_out, attn_sinks, lru_w_in, lru_conv_w,
           lru_conv_b, lru_w_a, lru_b_a, lru_w_x, lru_b_x, lru_lambda, lru_w_out):
    depth = norm_pre.shape[0]
    row = lambda v: v.reshape(v.shape[0], 1, D_MODEL)
    norm_pre, norm_post = row(norm_pre), row(norm_post)
    attn_params = (attn_w_in.astype(_BF16), attn_w_out.astype(_BF16), attn_sinks)
    lru_params = (lru_w_in.astype(_BF16), lru_conv_w, row(lru_conv_b),
                  lru_w_a.astype(_BF16), row(lru_b_a), lru_w_x.astype(_BF16), row(lru_b_x),
                  row(lru_lambda), lru_w_out.astype(_BF16))
    h = x.reshape(SEQ, D_MODEL)
    for layer in range(depth):
        j = layer // 2
        if layer % 2 == 0:
            h = _attn_layer(h, layer, j, norm_pre, norm_post, *attn_params)
        else:
            h = _lru_layer(h, layer, j, norm_pre, norm_post, *lru_params)
    return h.reshape(x.shape)
```

```python
import functools

import jax
import jax.numpy as jnp
from jax import lax
from jax.experimental import pallas as pl
from jax.experimental.pallas import tpu as pltpu

D_MODEL = 2048
SEQ = 8192
HEAD_DIM = 64
N_HEADS = 32
N_KV_HEADS = 4
GROUP = N_HEADS // N_KV_HEADS
BLOCK = 128
KV_W = N_KV_HEADS * HEAD_DIM
GATE_COL = D_MODEL + 2 * KV_W
LRU_BLOCK_W = 256
N_LRU_BLOCKS = D_MODEL // LRU_BLOCK_W
CONV_W = 4
C_RG = 8.0
NORM_EPS = 1e-6
MASK_VALUE = -1e30

LANES = 128
SUBLANES = 8
N_PAIRS = N_HEADS // 2
PAIRS_PER_KV = GROUP // 2

TM = 256
N_TILES = SEQ // TM
VMEM_LIMIT = 56 * 1024 * 1024

_F32 = jnp.float32
_BF16 = jnp.bfloat16


def _rms(x, g):
    ms = jnp.mean(x * x, axis=-1, keepdims=True)
    return x * lax.rsqrt(ms + NORM_EPS) * g


def _dot(a, b):
    return jnp.dot(a, b, preferred_element_type=_F32)


def _dot_nt(a, b):
    return lax.dot_general(a, b, (((1,), (1,)), ((), ())), preferred_element_type=_F32)


def _attn_kernel(j, sink_ref, h_ref, hprev_ref, gpre_ref, gpost_ref, win_ref, wout_ref,
                 out_ref, u_ref, q3_ref, sg3_ref, y3_ref, klo_ref, khi_ref, vlo_ref, vhi_ref):
    step = pl.program_id(0)
    cur = step % 2
    prv = 1 - cur
    nb = TM // BLOCK
    kv_refs = (klo_ref, khi_ref, vlo_ref, vhi_ref)

    @pl.when(step == 0)
    def _():
        q3_ref[...] = jnp.zeros(q3_ref.shape, _BF16)
        sg3_ref[...] = jnp.zeros(sg3_ref.shape, _F32)
        for ref in kv_refs:
            ref[...] = jnp.zeros(ref.shape, _BF16)

    u_ref[...] = _rms(h_ref[...], gpre_ref[...]).astype(_BF16)

    def project_kv():
        kv = _dot(u_ref[...], win_ref[:, D_MODEL:GATE_COL])
        lane_lo = lax.broadcasted_iota(jnp.int32, (TM, LANES), 1) < HEAD_DIM
        for t, (lo_ref, hi_ref) in enumerate(((klo_ref, khi_ref), (vlo_ref, vhi_ref))):
            for c in range(N_KV_HEADS // 2):
                col = t * KV_W + c * LANES
                orig = kv[:, col:col + LANES]
                sw = pltpu.roll(orig, HEAD_DIM, 1)
                new = slice(BLOCK, BLOCK + TM)
                lo_ref[cur, 2 * c, new, :] = jnp.where(lane_lo, orig, 0.0).astype(_BF16)
                hi_ref[cur, 2 * c, new, :] = jnp.where(lane_lo, 0.0, sw).astype(_BF16)
                lo_ref[cur, 2 * c + 1, new, :] = jnp.where(lane_lo, sw, 0.0).astype(_BF16)
                hi_ref[cur, 2 * c + 1, new, :] = jnp.where(lane_lo, 0.0, orig).astype(_BF16)
        for ref in kv_refs:
            ref[cur, :, 0:BLOCK, :] = ref[prv, :, TM:TM + BLOCK, :]

    stack = PAIRS_PER_KV * BLOCK
    qi = lax.broadcasted_iota(jnp.int32, (stack, BLOCK), 0) & (BLOCK - 1)
    kj = lax.broadcasted_iota(jnp.int32, (stack, BLOCK), 1)
    tri = kj <= qi
    prev_exists = (kj + ((step - 1) * TM - BLOCK)) >= 0

    def softmax_band(s_full, sink_col, b):
        s_prev = s_full[:, 0:BLOCK]
        s_cur = s_full[:, BLOCK:2 * BLOCK]
        if b == 0:
            s_prev = jnp.where(prev_exists, s_prev, MASK_VALUE)
        s = jnp.where(tri, s_cur, s_prev)
        m = jnp.maximum(jnp.max(s, axis=-1, keepdims=True), sink_col)
        e = jnp.exp(s - m)
        den = jnp.sum(e, axis=-1, keepdims=True) + jnp.exp(sink_col - m)
        pn = e * (1.0 / den)
        band = jnp.concatenate([jnp.where(tri, 0.0, pn), jnp.where(tri, pn, 0.0)], axis=1)
        return band.astype(_BF16)

    def sink_column(kvh, odd):
        cols = [jnp.full((BLOCK, 1), sink_ref[j, 2 * (PAIRS_PER_KV * kvh + i) + odd], _F32)
                for i in range(PAIRS_PER_KV)]
        return jnp.concatenate(cols, axis=0)

    for kvh in range(N_KV_HEADS):
        pairs = slice(PAIRS_PER_KV * kvh, PAIRS_PER_KV * (kvh + 1))
        scores = []
        for b in range(nb):
            rows = slice(b * BLOCK, (b + 1) * BLOCK)
            band = slice(b * BLOCK, (b + 2) * BLOCK)
            qs = q3_ref[prv, pairs, rows, :].reshape(stack, LANES)
            scores.append((_dot_nt(qs, klo_ref[prv, kvh, band, :]),
                           _dot_nt(qs, khi_ref[prv, kvh, band, :])))

        per_kv = GROUP * HEAD_DIM
        q = _dot(u_ref[...], win_ref[:, kvh * per_kv:(kvh + 1) * per_kv])
        gate = _dot(u_ref[...], win_ref[:, GATE_COL + kvh * per_kv:GATE_COL + (kvh + 1) * per_kv])
        for i in range(PAIRS_PER_KV):
            p = PAIRS_PER_KV * kvh + i
            q3_ref[cur, p] = (q[:, i * LANES:(i + 1) * LANES] * (1.0 / 8.0)).astype(_BF16)
            g = gate[:, i * LANES:(i + 1) * LANES]
            sg3_ref[cur, p] = g * jax.nn.sigmoid(g)

        sink_e = sink_column(kvh, 0)
        sink_o = sink_column(kvh, 1)
        for b in range(nb):
            rows = slice(b * BLOCK, (b + 1) * BLOCK)
            band = slice(b * BLOCK, (b + 2) * BLOCK)
            p_e = softmax_band(scores[b][0], sink_e, b)
            p_o = softmax_band(scores[b][1], sink_o, b)
            o2 = _dot(p_e, vlo_ref[prv, kvh, band, :]) + _dot(p_o, vhi_ref[prv, kvh, band, :])
            sg = sg3_ref[prv, pairs, rows, :].reshape(stack, LANES)
            y3_ref[pairs, rows, :] = (o2 * sg).astype(_BF16).reshape(PAIRS_PER_KV, BLOCK, LANES)

    y = jnp.concatenate([y3_ref[p] for p in range(N_PAIRS)], axis=1)
    yo = _dot(y, wout_ref[...])
    project_kv()
    out_ref[...] = hprev_ref[...] + _rms(yo, gpost_ref[...])


def _layer_spec(stacked, j):
    zeros = (0,) * (stacked.ndim - 1)
    return pl.BlockSpec((None,) + stacked.shape[1:], lambda i: (j,) + zeros,
                        pipeline_mode=pl.Buffered(1))


def _attn_layer(h, layer, j, norm_pre, norm_post, w_in, w_out, sinks):
    band_rows = BLOCK + TM
    lagged = lambda i: (jnp.maximum(i - 1, 0), 0)
    return pl.pallas_call(
        functools.partial(_attn_kernel, j),
        out_shape=jax.ShapeDtypeStruct((SEQ, D_MODEL), _F32),
        grid=(N_TILES + 1,),
        in_specs=[
            pl.BlockSpec(memory_space=pltpu.SMEM),
            pl.BlockSpec((TM, D_MODEL), lambda i: (jnp.minimum(i, N_TILES - 1), 0)),
            pl.BlockSpec((TM, D_MODEL), lagged),
            _layer_spec(norm_pre, layer),
            _layer_spec(norm_post, layer),
            _layer_spec(w_in, j),
            _layer_spec(w_out, j),
        ],
        out_specs=pl.BlockSpec((TM, D_MODEL), lagged),
        scratch_shapes=[
            pltpu.VMEM((TM, D_MODEL), _BF16),
            pltpu.VMEM((2, N_PAIRS, TM, LANES), _BF16),
            pltpu.VMEM((2, N_PAIRS, TM, LANES), _F32),
            pltpu.VMEM((N_PAIRS, TM, LANES), _BF16),
            pltpu.VMEM((2, N_KV_HEADS, band_rows, LANES), _BF16),
            pltpu.VMEM((2, N_KV_HEADS, band_rows, LANES), _BF16),
            pltpu.VMEM((2, N_KV_HEADS, band_rows, LANES), _BF16),
            pltpu.VMEM((2, N_KV_HEADS, band_rows, LANES), _BF16),
        ],
        compiler_params=pltpu.CompilerParams(
            dimension_semantics=("arbitrary",), vmem_limit_bytes=VMEM_LIMIT),
        name="attn_layer",
    )(sinks, h, h, norm_pre, norm_post, w_in, w_out)


SEG = TM // SUBLANES
CONV_TAIL = (CONV_W - 1) * SUBLANES


def _sublane_scan(a, b, row):
    for d in (1, 2, 4):
        keep = row >= d
        a_sh = jnp.where(keep, pltpu.roll(a, d, 0), 1.0)
        b_sh = jnp.where(keep, pltpu.roll(b, d, 0), 0.0)
        b = a * b_sh + b
        a = a * a_sh
    return a, b


def _lru_kernel(h_ref, hprev_ref, gpre_ref, gpost_ref, win_ref, cw_ref, cb_ref, wa_ref, ba_ref,
                wx_ref, bx_ref, lam_ref, wout_ref, out_ref,
                u_ref, x_ref, tail_ref, y_ref, ynat_ref, yo_ref, sp_ref, hc_ref, perm_ref,
                unperm_ref):
    step = pl.program_id(0)
    cur = step % 2
    prv = 1 - cur

    @pl.when(step == 0)
    def _():
        tail_ref[...] = jnp.zeros(tail_ref.shape, _F32)
        hc_ref[...] = jnp.zeros(hc_ref.shape, _F32)
        y_ref[...] = jnp.zeros(y_ref.shape, _BF16)
        nl = -lam_ref[...]
        softplus = jnp.maximum(nl, 0.0) + jnp.log1p(jnp.exp(-jnp.abs(nl)))
        sp_ref[...] = jnp.broadcast_to(softplus, sp_ref.shape)
        ri = lax.broadcasted_iota(jnp.int32, (TM, TM), 0)
        ci = lax.broadcasted_iota(jnp.int32, (TM, TM), 1)
        perm_ref[...] = (ci == (ri % SUBLANES) * SEG + ri // SUBLANES).astype(_BF16)
        unperm_ref[...] = (ci == (ri % SEG) * SUBLANES + ri // SEG).astype(_BF16)

    row = lax.broadcasted_iota(jnp.int32, (SUBLANES, LRU_BLOCK_W), 0)
    first_segment = row == 0

    def block_cols(blk):
        return slice(blk * LRU_BLOCK_W, (blk + 1) * LRU_BLOCK_W)

    def project_x(blk):
        cs = block_cols(blk)
        x_ref[:, cs] = _dot(u_ref[...], win_ref[:, cs])

    def out_project(blk):
        cs = block_cols(blk)
        yo_ref[:, cs] = _dot(ynat_ref[...], wout_ref[:, cs])

    ynat_ref[...] = _dot(unperm_ref[...], y_ref[prv]).astype(_BF16)
    out_project(0)
    out_project(1)
    u_nat = _rms(h_ref[...], gpre_ref[...]).astype(_BF16)
    u_ref[...] = _dot(perm_ref[...], u_nat).astype(_BF16)
    project_x(0)
    for blk in range(N_LRU_BLOCKS):
        cs = block_cols(blk)
        if blk + 1 < N_LRU_BLOCKS:
            project_x(blk + 1)
        if blk + 2 < N_LRU_BLOCKS:
            out_project(blk + 2)
        gate = _dot(u_ref[...], win_ref[:, D_MODEL + blk * LRU_BLOCK_W:D_MODEL + (blk + 1) * LRU_BLOCK_W])

        x = x_ref[:, cs]
        tail_cur = x[TM - CONV_TAIL:]
        tail_prev = tail_ref[:, cs]
        tail_ref[:, cs] = tail_cur
        wrap = []
        for i in range(CONV_W - 1):
            rows = slice(i * SUBLANES, (i + 1) * SUBLANES)
            wrap.append(jnp.where(first_segment, pltpu.roll(tail_prev[rows], 1, 0),
                                  pltpu.roll(tail_cur[rows], 1, 0)))
        xc = cb_ref[0:1, cs] + cw_ref[CONV_W - 1:CONV_W, cs] * x
        for shift in range(1, CONV_W):
            tap = CONV_W - 1 - shift
            shifted = jnp.concatenate(wrap[CONV_W - 1 - shift:] + [x[:TM - shift * SUBLANES]], axis=0)
            xc = xc + cw_ref[tap:tap + 1, cs] * shifted
        xcb = xc.astype(_BF16)
        r = jax.nn.sigmoid(_dot(xcb, wa_ref[blk]) + ba_ref[0:1, cs])
        ig = jax.nn.sigmoid(_dot(xcb, wx_ref[blk]) + bx_ref[0:1, cs])
        sg = gate * jax.nn.sigmoid(gate)

        nla = (C_RG * r) * sp_ref[0:1, cs]
        a = jnp.exp(-nla)
        z = jnp.tanh(nla) * (a * a + 1.0)
        mult = jnp.where(z > 0.0, z * lax.rsqrt(z), 0.0)
        b = mult * (ig * xc)

        acc_a = a[0:SUBLANES]
        acc_h = b[0:SUBLANES]
        prods, local = [acc_a], [acc_h]
        for g in range(1, SEG):
            rows = slice(g * SUBLANES, (g + 1) * SUBLANES)
            acc_h = a[rows] * acc_h + b[rows]
            acc_a = a[rows] * acc_a
            prods.append(acc_a)
            local.append(acc_h)
        carry_in = hc_ref[:, cs]
        seg_a, seg_h = _sublane_scan(acc_a, acc_h, row)
        seg_end = seg_h + seg_a * carry_in
        entering = jnp.where(first_segment, carry_in, pltpu.roll(seg_end, 1, 0))
        hc_ref[:, cs] = jnp.broadcast_to(seg_end[SUBLANES - 1:SUBLANES, :], (SUBLANES, LRU_BLOCK_W))
        ys = [(local[g] + prods[g] * entering) * sg[g * SUBLANES:(g + 1) * SUBLANES]
              for g in range(SEG)]
        y = jnp.concatenate(ys, axis=0).astype(_BF16)
        y_ref[cur, :, cs] = y
        if blk + 2 == N_LRU_BLOCKS:
            out_ref[...] = hprev_ref[...] + _rms(yo_ref[...], gpost_ref[...])


def _lru_layer(h, layer, j, norm_pre, norm_post, *params):
    lagged = lambda i: (jnp.maximum(i - 1, 0), 0)
    return pl.pallas_call(
        _lru_kernel,
        out_shape=jax.ShapeDtypeStruct((SEQ, D_MODEL), _F32),
        grid=(N_TILES + 1,),
        in_specs=[
            pl.BlockSpec((TM, D_MODEL), lambda i: (jnp.minimum(i, N_TILES - 1), 0)),
            pl.BlockSpec((TM, D_MODEL), lagged),
            _layer_spec(norm_pre, layer),
            _layer_spec(norm_post, layer),
        ] + [_layer_spec(p, j) for p in params],
        out_specs=pl.BlockSpec((TM, D_MODEL), lagged),
        scratch_shapes=[
            pltpu.VMEM((TM, D_MODEL), _BF16),
            pltpu.VMEM((TM, D_MODEL), _F32),
            pltpu.VMEM((CONV_TAIL, D_MODEL), _F32),
            pltpu.VMEM((2, TM, D_MODEL), _BF16),
            pltpu.VMEM((TM, D_MODEL), _BF16),
            pltpu.VMEM((TM, D_MODEL), _F32),
            pltpu.VMEM((SUBLANES, D_MODEL), _F32),
            pltpu.VMEM((SUBLANES, D_MODEL), _F32),
            pltpu.VMEM((TM, TM), _BF16),
            pltpu.VMEM((TM, TM), _BF16),
        ],
        compiler_params=pltpu.CompilerParams(
            dimension_semantics=("arbitrary",), vmem_limit_bytes=VMEM_LIMIT),
        name="lru_layer",
    )(h, h, norm_pre, norm_post, *params)


def kernel(x, norm_pre, norm_post, attn_w_in, attn_w_out, attn_sinks, lru_w_in, lru_conv_w,
           lru_conv_b, lru_w_a, lru_b_a, lru_w_x, lru_b_x, lru_lambda, lru_w_out):
    depth = norm_pre.shape[0]
    row = lambda v: v.reshape(v.shape[0], 1, D_MODEL)
    norm_pre, norm_post = row(norm_pre), row(norm_post)
    attn_params = (attn_w_in.astype(_BF16), attn_w_out.astype(_BF16), attn_sinks)
    lru_params = (lru_w_in.astype(_BF16), lru_conv_w, row(lru_conv_b),
                  lru_w_a.astype(_BF16), row(lru_b_a), lru_w_x.astype(_BF16), row(lru_b_x),
                  row(lru_lambda), lru_w_out.astype(_BF16))
    h = x.reshape(SEQ, D_MODEL)
    for layer in range(depth):
        j = layer // 2
        if layer % 2 == 0:
            h = _attn_layer(h, layer, j, norm_pre, norm_post, *attn_params)
        else:
            h = _lru_layer(h, layer, j, norm_pre, norm_post, *lru_params)
    return h.reshape(x.shape)
```

```python
import functools

import jax
import jax.numpy as jnp
from jax import lax
from jax.experimental import pallas as pl
from jax.experimental.pallas import tpu as pltpu

D_MODEL = 2048
SEQ = 8192
HEAD_DIM = 64
N_HEADS = 32
N_KV_HEADS = 4
GROUP = N_HEADS // N_KV_HEADS
BLOCK = 128
KV_W = N_KV_HEADS * HEAD_DIM
GATE_COL = D_MODEL + 2 * KV_W
LRU_BLOCK_W = 256
N_LRU_BLOCKS = D_MODEL // LRU_BLOCK_W
CONV_W = 4
C_RG = 8.0
NORM_EPS = 1e-6
MASK_VALUE = -1e30

LANES = 128
SUBLANES = 8
N_PAIRS = N_HEADS // 2
PAIRS_PER_KV = GROUP // 2

TM = 256
N_TILES = SEQ // TM
VMEM_LIMIT = 56 * 1024 * 1024

_F32 = jnp.float32
_BF16 = jnp.bfloat16


def _rms(x, g):
    ms = jnp.mean(x * x, axis=-1, keepdims=True)
    return x * lax.rsqrt(ms + NORM_EPS) * g


def _dot(a, b):
    return jnp.dot(a, b, preferred_element_type=_F32)


def _dot_nt(a, b):
    return lax.dot_general(a, b, (((1,), (1,)), ((), ())), preferred_element_type=_F32)


def _attn_kernel(j, sink_ref, h_ref, hprev_ref, gpre_ref, gpost_ref, win_ref, wout_ref,
                 out_ref, u_ref, q3_ref, sg3_ref, y3_ref, klo_ref, khi_ref, vt_ref):
    step = pl.program_id(0)
    cur = step % 2
    prv = 1 - cur
    nb = TM // BLOCK

    @pl.when(step == 0)
    def _():
        q3_ref[...] = jnp.zeros(q3_ref.shape, _BF16)
        sg3_ref[...] = jnp.zeros(sg3_ref.shape, _F32)
        for ref in (klo_ref, khi_ref, vt_ref):
            ref[...] = jnp.zeros(ref.shape, _BF16)

    u_ref[...] = _rms(h_ref[...], gpre_ref[...]).astype(_BF16)

    def project_kv():
        kv = _dot(u_ref[...], win_ref[:, D_MODEL:GATE_COL])
        lane_lo = lax.broadcasted_iota(jnp.int32, (TM, LANES), 1) < HEAD_DIM
        new = slice(BLOCK, BLOCK + TM)
        for c in range(N_KV_HEADS // 2):
            orig = kv[:, c * LANES:(c + 1) * LANES]
            sw = pltpu.roll(orig, HEAD_DIM, 1)
            klo_ref[cur, 2 * c, new, :] = jnp.where(lane_lo, orig, 0.0).astype(_BF16)
            khi_ref[cur, 2 * c, new, :] = jnp.where(lane_lo, 0.0, sw).astype(_BF16)
            klo_ref[cur, 2 * c + 1, new, :] = jnp.where(lane_lo, sw, 0.0).astype(_BF16)
            khi_ref[cur, 2 * c + 1, new, :] = jnp.where(lane_lo, 0.0, orig).astype(_BF16)
        for ref in (klo_ref, khi_ref):
            ref[cur, :, 0:BLOCK, :] = ref[prv, :, TM:TM + BLOCK, :]
        vt = kv[:, KV_W:2 * KV_W].T.astype(_BF16)
        vt_ref[cur, :, new] = vt
        vt_ref[cur, :, 0:BLOCK] = vt_ref[prv, :, TM:TM + BLOCK]

    stack = PAIRS_PER_KV * BLOCK
    kj = lax.broadcasted_iota(jnp.int32, (BLOCK, stack), 0)
    qi = lax.broadcasted_iota(jnp.int32, (BLOCK, stack), 1) & (BLOCK - 1)
    tri = kj <= qi
    prev_exists = (kj + ((step - 1) * TM - BLOCK)) >= 0
    chunks = BLOCK // SUBLANES

    def reduce_rows(x3, op):
        r = x3[0]
        for c in range(1, chunks):
            r = op(r, x3[c])
        for d in (1, 2, 4):
            r = op(r, pltpu.roll(r, d, 0))
        return r

    def softmax_band(s_full, sink_row, b):
        s_prev = s_full[0:BLOCK]
        s_cur = s_full[BLOCK:2 * BLOCK]
        if b == 0:
            s_prev = jnp.where(prev_exists, s_prev, MASK_VALUE)
        s3 = jnp.where(tri, s_cur, s_prev).reshape(chunks, SUBLANES, stack)
        m = jnp.maximum(reduce_rows(s3, jnp.maximum), sink_row)
        e3 = jnp.exp(s3 - m[None])
        den = reduce_rows(e3, jnp.add) + jnp.exp(sink_row - m)
        pn = (e3 * (1.0 / den)[None]).reshape(BLOCK, stack)
        band = jnp.concatenate([jnp.where(tri, 0.0, pn), jnp.where(tri, pn, 0.0)], axis=0)
        return band.astype(_BF16)

    def sink_row(kvh, odd):
        cols = [jnp.full((SUBLANES, BLOCK), sink_ref[j, 2 * (PAIRS_PER_KV * kvh + i) + odd], _F32)
                for i in range(PAIRS_PER_KV)]
        return jnp.concatenate(cols, axis=1)

    for kvh in range(N_KV_HEADS):
        pairs = slice(PAIRS_PER_KV * kvh, PAIRS_PER_KV * (kvh + 1))
        scores = []
        for b in range(nb):
            rows = slice(b * BLOCK, (b + 1) * BLOCK)
            band = slice(b * BLOCK, (b + 2) * BLOCK)
            qs = q3_ref[prv, pairs, rows, :].reshape(stack, LANES)
            scores.append((_dot_nt(klo_ref[prv, kvh, band, :], qs),
                           _dot_nt(khi_ref[prv, kvh, band, :], qs)))

        per_kv = GROUP * HEAD_DIM
        q = _dot(u_ref[...], win_ref[:, kvh * per_kv:(kvh + 1) * per_kv])
        gate = _dot(u_ref[...], win_ref[:, GATE_COL + kvh * per_kv:GATE_COL + (kvh + 1) * per_kv])
        for i in range(PAIRS_PER_KV):
            p = PAIRS_PER_KV * kvh + i
            q3_ref[cur, p] = (q[:, i * LANES:(i + 1) * LANES] * (1.0 / 8.0)).astype(_BF16)
            g = gate[:, i * LANES:(i + 1) * LANES]
            sg3_ref[cur, p] = g * jax.nn.sigmoid(g)

        sink_e = sink_row(kvh, 0)
        sink_o = sink_row(kvh, 1)
        for b in range(nb):
            rows = slice(b * BLOCK, (b + 1) * BLOCK)
            band = slice(b * BLOCK, (b + 2) * BLOCK)
            vt = vt_ref[prv, kvh * HEAD_DIM:(kvh + 1) * HEAD_DIM, band]
            o_e = _dot(vt, softmax_band(scores[b][0], sink_e, b))
            o_o = _dot(vt, softmax_band(scores[b][1], sink_o, b))
            o2 = jnp.concatenate([o_e, o_o], axis=0).T
            sg = sg3_ref[prv, pairs, rows, :].reshape(stack, LANES)
            y3_ref[pairs, rows, :] = (o2 * sg).astype(_BF16).reshape(PAIRS_PER_KV, BLOCK, LANES)

    project_kv()

    y = jnp.concatenate([y3_ref[p] for p in range(N_PAIRS)], axis=1)
    yo = _dot(y, wout_ref[...])
    out_ref[...] = hprev_ref[...] + _rms(yo, gpost_ref[...])


def _layer_spec(stacked, j):
    zeros = (0,) * (stacked.ndim - 1)
    return pl.BlockSpec((None,) + stacked.shape[1:], lambda i: (j,) + zeros,
                        pipeline_mode=pl.Buffered(1))


def _attn_layer(h, layer, j, norm_pre, norm_post, w_in, w_out, sinks):
    band_rows = BLOCK + TM
    lagged = lambda i: (jnp.maximum(i - 1, 0), 0)
    return pl.pallas_call(
        functools.partial(_attn_kernel, j),
        out_shape=jax.ShapeDtypeStruct((SEQ, D_MODEL), _F32),
        grid=(N_TILES + 1,),
        in_specs=[
            pl.BlockSpec(memory_space=pltpu.SMEM),
            pl.BlockSpec((TM, D_MODEL), lambda i: (jnp.minimum(i, N_TILES - 1), 0)),
            pl.BlockSpec((TM, D_MODEL), lagged),
            _layer_spec(norm_pre, layer),
            _layer_spec(norm_post, layer),
            _layer_spec(w_in, j),
            _layer_spec(w_out, j),
        ],
        out_specs=pl.BlockSpec((TM, D_MODEL), lagged),
        scratch_shapes=[
            pltpu.VMEM((TM, D_MODEL), _BF16),
            pltpu.VMEM((2, N_PAIRS, TM, LANES), _BF16),
            pltpu.VMEM((2, N_PAIRS, TM, LANES), _F32),
            pltpu.VMEM((N_PAIRS, TM, LANES), _BF16),
            pltpu.VMEM((2, N_KV_HEADS, band_rows, LANES), _BF16),
            pltpu.VMEM((2, N_KV_HEADS, band_rows, LANES), _BF16),
            pltpu.VMEM((2, KV_W, band_rows), _BF16),
        ],
        compiler_params=pltpu.CompilerParams(
            dimension_semantics=("arbitrary",), vmem_limit_bytes=VMEM_LIMIT),
        name="attn_layer",
    )(sinks, h, h, norm_pre, norm_post, w_in, w_out)


SEG = TM // SUBLANES
CONV_TAIL = (CONV_W - 1) * SUBLANES


def _sublane_scan(a, b, row):
    for d in (1, 2, 4):
        keep = row >= d
        a_sh = jnp.where(keep, pltpu.roll(a, d, 0), 1.0)
        b_sh = jnp.where(keep, pltpu.roll(b, d, 0), 0.0)
        b = a * b_sh + b
        a = a * a_sh
    return a, b


def _lru_kernel(h_ref, hprev_ref, gpre_ref, gpost_ref, win_ref, cw_ref, cb_ref, wa_ref, ba_ref,
                wx_ref, bx_ref, lam_ref, wout_ref, out_ref,
                u_ref, x_ref, tail_ref, y_ref, ynat_ref, yo_ref, sp_ref, hc_ref, perm_ref,
                unperm_ref):
    step = pl.program_id(0)
    cur = step % 2
    prv = 1 - cur

    @pl.when(step == 0)
    def _():
        tail_ref[...] = jnp.zeros(tail_ref.shape, _F32)
        hc_ref[...] = jnp.zeros(hc_ref.shape, _F32)
        y_ref[...] = jnp.zeros(y_ref.shape, _BF16)
        nl = -lam_ref[...]
        softplus = jnp.maximum(nl, 0.0) + jnp.log1p(jnp.exp(-jnp.abs(nl)))
        sp_ref[...] = jnp.broadcast_to(softplus, sp_ref.shape)
        ri = lax.broadcasted_iota(jnp.int32, (TM, TM), 0)
        ci = lax.broadcasted_iota(jnp.int32, (TM, TM), 1)
        perm_ref[...] = (ci == (ri % SUBLANES) * SEG + ri // SUBLANES).astype(_BF16)
        unperm_ref[...] = (ci == (ri % SEG) * SUBLANES + ri // SEG).astype(_BF16)

    row = lax.broadcasted_iota(jnp.int32, (SUBLANES, LRU_BLOCK_W), 0)
    first_segment = row == 0

    def block_cols(blk):
        return slice(blk * LRU_BLOCK_W, (blk + 1) * LRU_BLOCK_W)

    def project_x(blk):
        cs = block_cols(blk)
        x_ref[:, cs] = _dot(u_ref[...], win_ref[:, cs])

    def out_project(blk):
        cs = block_cols(blk)
        yo_ref[:, cs] = _dot(ynat_ref[...], wout_ref[:, cs])

    ynat_ref[...] = _dot(unperm_ref[...], y_ref[prv]).astype(_BF16)
    out_project(0)
    out_project(1)
    u_nat = _rms(h_ref[...], gpre_ref[...]).astype(_BF16)
    u_ref[...] = _dot(perm_ref[...], u_nat).astype(_BF16)
    project_x(0)
    for blk in range(N_LRU_BLOCKS):
        cs = block_cols(blk)
        if blk + 1 < N_LRU_BLOCKS:
            project_x(blk + 1)
        if blk + 2 < N_LRU_BLOCKS:
            out_project(blk + 2)
        gate = _dot(u_ref[...], win_ref[:, D_MODEL + blk * LRU_BLOCK_W:D_MODEL + (blk + 1) * LRU_BLOCK_W])

        x = x_ref[:, cs]
        tail_cur = x[TM - CONV_TAIL:]
        tail_prev = tail_ref[:, cs]
        tail_ref[:, cs] = tail_cur
        wrap = []
        for i in range(CONV_W - 1):
            rows = slice(i * SUBLANES, (i + 1) * SUBLANES)
            wrap.append(jnp.where(first_segment, pltpu.roll(tail_prev[rows], 1, 0),
                                  pltpu.roll(tail_cur[rows], 1, 0)))
        xc = cb_ref[0:1, cs] + cw_ref[CONV_W - 1:CONV_W, cs] * x
        for shift in range(1, CONV_W):
            tap = CONV_W - 1 - shift
            shifted = jnp.concatenate(wrap[CONV_W - 1 - shift:] + [x[:TM - shift * SUBLANES]], axis=0)
            xc = xc + cw_ref[tap:tap + 1, cs] * shifted
        xcb = xc.astype(_BF16)
        r = jax.nn.sigmoid(_dot(xcb, wa_ref[blk]) + ba_ref[0:1, cs])
        ig = jax.nn.sigmoid(_dot(xcb, wx_ref[blk]) + bx_ref[0:1, cs])
        sg = gate * jax.nn.sigmoid(gate)

        nla = (C_RG * r) * sp_ref[0:1, cs]
        a = jnp.exp(-nla)
        z = jnp.tanh(nla) * (a * a + 1.0)
        mult = jnp.where(z > 0.0, z * lax.rsqrt(z), 0.0)
        b = mult * (ig * xc)

        acc_a = a[0:SUBLANES]
        acc_h = b[0:SUBLANES]
        prods, local = [acc_a], [acc_h]
        for g in range(1, SEG):
            rows = slice(g * SUBLANES, (g + 1) * SUBLANES)
            acc_h = a[rows] * acc_h + b[rows]
            acc_a = a[rows] * acc_a
            prods.append(acc_a)
            local.append(acc_h)
        carry_in = hc_ref[:, cs]
        seg_a, seg_h = _sublane_scan(acc_a, acc_h, row)
        seg_end = seg_h + seg_a * carry_in
        entering = jnp.where(first_segment, carry_in, pltpu.roll(seg_end, 1, 0))
        hc_ref[:, cs] = jnp.broadcast_to(seg_end[SUBLANES - 1:SUBLANES, :], (SUBLANES, LRU_BLOCK_W))
        ys = [(local[g] + prods[g] * entering) * sg[g * SUBLANES:(g + 1) * SUBLANES]
              for g in range(SEG)]
        y = jnp.concatenate(ys, axis=0).astype(_BF16)
        y_ref[cur, :, cs] = y
        if blk + 2 == N_LRU_BLOCKS:
            out_ref[...] = hprev_ref[...] + _rms(yo_ref[...], gpost_ref[...])


def _lru_layer(h, layer, j, norm_pre, norm_post, *params):
    lagged = lambda i: (jnp.maximum(i - 1, 0), 0)
    return pl.pallas_call(
        _lru_kernel,
        out_shape=jax.ShapeDtypeStruct((SEQ, D_MODEL), _F32),
        grid=(N_TILES + 1,),
        in_specs=[
            pl.BlockSpec((TM, D_MODEL), lambda i: (jnp.minimum(i, N_TILES - 1), 0)),
            pl.BlockSpec((TM, D_MODEL), lagged),
            _layer_spec(norm_pre, layer),
            _layer_spec(norm_post, layer),
        ] + [_layer_spec(p, j) for p in params],
        out_specs=pl.BlockSpec((TM, D_MODEL), lagged),
        scratch_shapes=[
            pltpu.VMEM((TM, D_MODEL), _BF16),
            pltpu.VMEM((TM, D_MODEL), _F32),
            pltpu.VMEM((CONV_TAIL, D_MODEL), _F32),
            pltpu.VMEM((2, TM, D_MODEL), _BF16),
            pltpu.VMEM((TM, D_MODEL), _BF16),
            pltpu.VMEM((TM, D_MODEL), _F32),
            pltpu.VMEM((SUBLANES, D_MODEL), _F32),
            pltpu.VMEM((SUBLANES, D_MODEL), _F32),
            pltpu.VMEM((TM, TM), _BF16),
            pltpu.VMEM((TM, TM), _BF16),
        ],
        compiler_params=pltpu.CompilerParams(
            dimension_semantics=("arbitrary",), vmem_limit_bytes=VMEM_LIMIT),
        name="lru_layer",
    )(h, h, norm_pre, norm_post, *params)


def kernel(x, norm_pre, norm_post, attn_w_in, attn_w_out, attn_sinks, lru_w_in, lru_conv_w,
           lru_conv_b, lru_w_a, lru_b_a, lru_w_x, lru_b_x, lru_lambda, lru_w_out):
    depth = norm_pre.shape[0]
    row = lambda v: v.reshape(v.shape[0], 1, D_MODEL)
    norm_pre, norm_post = row(norm_pre), row(norm_post)
    attn_params = (attn_w_in.astype(_BF16), attn_w_out.astype(_BF16), attn_sinks)
    lru_params = (lru_w_in.astype(_BF16), lru_conv_w, row(lru_conv_b),
                  lru_w_a.astype(_BF16), row(lru_b_a), lru_w_x.astype(_BF16), row(lru_b_x),
                  row(lru_lambda), lru_w_out.astype(_BF16))
    h = x.reshape(SEQ, D_MODEL)
    for layer in range(depth):
        j = layer // 2
        if layer % 2 == 0:
            h = _attn_layer(h, layer, j, norm_pre, norm_post, *attn_params)
        else:
            h = _lru_layer(h, layer, j, norm_pre, norm_post, *lru_params)
    return h.reshape(x.shape)
```

```python
import functools

import jax
import jax.numpy as jnp
from jax import lax
from jax.experimental import pallas as pl
from jax.experimental.pallas import tpu as pltpu

D_MODEL = 2048
SEQ = 8192
HEAD_DIM = 64
N_HEADS = 32
N_KV_HEADS = 4
GROUP = N_HEADS // N_KV_HEADS
BLOCK = 128
KV_W = N_KV_HEADS * HEAD_DIM
GATE_COL = D_MODEL + 2 * KV_W
LRU_BLOCK_W = 256
N_LRU_BLOCKS = D_MODEL // LRU_BLOCK_W
CONV_W = 4
C_RG = 8.0
NORM_EPS = 1e-6
MASK_VALUE = -1e30

LANES = 128
SUBLANES = 8
N_PAIRS = N_HEADS // 2
PAIRS_PER_KV = GROUP // 2

TM = 256
N_TILES = SEQ // TM
VMEM_LIMIT = 56 * 1024 * 1024

_F32 = jnp.float32
_BF16 = jnp.bfloat16


def _rms(x, g):
    ms = jnp.mean(x * x, axis=-1, keepdims=True)
    return x * lax.rsqrt(ms + NORM_EPS) * g


def _dot(a, b):
    return jnp.dot(a, b, preferred_element_type=_F32)


def _dot_nt(a, b):
    return lax.dot_general(a, b, (((1,), (1,)), ((), ())), preferred_element_type=_F32)


def _staggered(step, body):
    pl.when(step == 0)(lambda: body(True, False))
    pl.when(jnp.logical_and(step > 0, step < N_TILES))(lambda: body(True, True))
    pl.when(step == N_TILES)(lambda: body(False, True))


def _attn_kernel(j, sink_ref, h_ref, hprev_ref, gpre_ref, gpost_ref, win_ref, wout_ref,
                 out_ref, u_ref, q3_ref, sg3_ref, y3_ref, klo_ref, khi_ref, vt_ref):
    step = pl.program_id(0)
    cur = step % 2
    prv = 1 - cur
    nb = TM // BLOCK
    stack = PAIRS_PER_KV * BLOCK
    chunks = BLOCK // SUBLANES
    per_kv = GROUP * HEAD_DIM

    def project_kv(first):
        kv = _dot(u_ref[...], win_ref[:, D_MODEL:GATE_COL])
        lane_lo = lax.broadcasted_iota(jnp.int32, (TM, LANES), 1) < HEAD_DIM
        new = slice(BLOCK, BLOCK + TM)
        for c in range(N_KV_HEADS // 2):
            orig = kv[:, c * LANES:(c + 1) * LANES]
            sw = pltpu.roll(orig, HEAD_DIM, 1)
            klo_ref[cur, 2 * c, new, :] = jnp.where(lane_lo, orig, 0.0).astype(_BF16)
            khi_ref[cur, 2 * c, new, :] = jnp.where(lane_lo, 0.0, sw).astype(_BF16)
            klo_ref[cur, 2 * c + 1, new, :] = jnp.where(lane_lo, sw, 0.0).astype(_BF16)
            khi_ref[cur, 2 * c + 1, new, :] = jnp.where(lane_lo, 0.0, orig).astype(_BF16)
        vt_ref[cur, :, new] = kv[:, KV_W:2 * KV_W].T.astype(_BF16)
        for ref in (klo_ref, khi_ref):
            prev = jnp.zeros((N_KV_HEADS, BLOCK, LANES), _BF16) if first else ref[prv, :, TM:TM + BLOCK, :]
            ref[cur, :, 0:BLOCK, :] = prev
        vt_ref[cur, :, 0:BLOCK] = (jnp.zeros((KV_W, BLOCK), _BF16) if first
                                   else vt_ref[prv, :, TM:TM + BLOCK])

    def reduce_rows(x3, op):
        r = x3[0]
        for c in range(1, chunks):
            r = op(r, x3[c])
        for d in (1, 2, 4):
            r = op(r, pltpu.roll(r, d, 0))
        return r

    def softmax_band(s_full, sink_row, b, tri, prev_exists):
        s_prev = s_full[0:BLOCK]
        s_cur = s_full[BLOCK:2 * BLOCK]
        if b == 0:
            s_prev = jnp.where(prev_exists, s_prev, MASK_VALUE)
        s3 = jnp.where(tri, s_cur, s_prev).reshape(chunks, SUBLANES, stack)
        m = jnp.maximum(reduce_rows(s3, jnp.maximum), sink_row)
        e3 = jnp.exp(s3 - m[None])
        den = reduce_rows(e3, jnp.add) + jnp.exp(sink_row - m)
        pn = (e3 * (1.0 / den)[None]).reshape(BLOCK, stack)
        band = jnp.concatenate([jnp.where(tri, 0.0, pn), jnp.where(tri, pn, 0.0)], axis=0)
        return band.astype(_BF16)

    def sink_row(kvh, odd):
        cols = [jnp.full((SUBLANES, BLOCK), sink_ref[j, 2 * (PAIRS_PER_KV * kvh + i) + odd], _F32)
                for i in range(PAIRS_PER_KV)]
        return jnp.concatenate(cols, axis=1)

    def body(project, attend):
        if project:
            u_ref[...] = _rms(h_ref[...], gpre_ref[...]).astype(_BF16)
        if attend:
            kj = lax.broadcasted_iota(jnp.int32, (BLOCK, stack), 0)
            qi = lax.broadcasted_iota(jnp.int32, (BLOCK, stack), 1) & (BLOCK - 1)
            tri = kj <= qi
            prev_exists = (kj + ((step - 1) * TM - BLOCK)) >= 0

        for kvh in range(N_KV_HEADS):
            pairs = slice(PAIRS_PER_KV * kvh, PAIRS_PER_KV * (kvh + 1))
            if attend:
                scores = []
                for b in range(nb):
                    rows = slice(b * BLOCK, (b + 1) * BLOCK)
                    band = slice(b * BLOCK, (b + 2) * BLOCK)
                    qs = q3_ref[prv, pairs, rows, :].reshape(stack, LANES)
                    scores.append((_dot_nt(klo_ref[prv, kvh, band, :], qs),
                                   _dot_nt(khi_ref[prv, kvh, band, :], qs)))

            if project:
                q = _dot(u_ref[...], win_ref[:, kvh * per_kv:(kvh + 1) * per_kv])
                gate = _dot(u_ref[...],
                            win_ref[:, GATE_COL + kvh * per_kv:GATE_COL + (kvh + 1) * per_kv])
                for i in range(PAIRS_PER_KV):
                    p = PAIRS_PER_KV * kvh + i
                    q3_ref[cur, p] = (q[:, i * LANES:(i + 1) * LANES] * (1.0 / 8.0)).astype(_BF16)
                    g = gate[:, i * LANES:(i + 1) * LANES]
                    sg3_ref[cur, p] = g * jax.nn.sigmoid(g)

            if attend:
                sink_e = sink_row(kvh, 0)
                sink_o = sink_row(kvh, 1)
                for b in range(nb):
                    rows = slice(b * BLOCK, (b + 1) * BLOCK)
                    band = slice(b * BLOCK, (b + 2) * BLOCK)
                    vt = vt_ref[prv, kvh * HEAD_DIM:(kvh + 1) * HEAD_DIM, band]
                    o_e = _dot(vt, softmax_band(scores[b][0], sink_e, b, tri, prev_exists))
                    o_o = _dot(vt, softmax_band(scores[b][1], sink_o, b, tri, prev_exists))
                    o2 = jnp.concatenate([o_e, o_o], axis=0).T
                    sg = sg3_ref[prv, pairs, rows, :].reshape(stack, LANES)
                    y3_ref[pairs, rows, :] = (o2 * sg).astype(_BF16).reshape(PAIRS_PER_KV, BLOCK, LANES)

        if project:
            project_kv(first=not attend)

        if attend:
            y = jnp.concatenate([y3_ref[p] for p in range(N_PAIRS)], axis=1)
            yo = _dot(y, wout_ref[...])
            out_ref[...] = hprev_ref[...] + _rms(yo, gpost_ref[...])

    _staggered(step, body)


def _layer_spec(stacked, j):
    zeros = (0,) * (stacked.ndim - 1)
    return pl.BlockSpec((None,) + stacked.shape[1:], lambda i: (j,) + zeros,
                        pipeline_mode=pl.Buffered(1))


def _attn_layer(h, layer, j, norm_pre, norm_post, w_in, w_out, sinks):
    band_rows = BLOCK + TM
    lagged = lambda i: (jnp.maximum(i - 1, 0), 0)
    return pl.pallas_call(
        functools.partial(_attn_kernel, j),
        out_shape=jax.ShapeDtypeStruct((SEQ, D_MODEL), _F32),
        grid=(N_TILES + 1,),
        in_specs=[
            pl.BlockSpec(memory_space=pltpu.SMEM),
            pl.BlockSpec((TM, D_MODEL), lambda i: (jnp.minimum(i, N_TILES - 1), 0)),
            pl.BlockSpec((TM, D_MODEL), lagged),
            _layer_spec(norm_pre, layer),
            _layer_spec(norm_post, layer),
            _layer_spec(w_in, j),
            _layer_spec(w_out, j),
        ],
        out_specs=pl.BlockSpec((TM, D_MODEL), lagged),
        scratch_shapes=[
            pltpu.VMEM((TM, D_MODEL), _BF16),
            pltpu.VMEM((2, N_PAIRS, TM, LANES), _BF16),
            pltpu.VMEM((2, N_PAIRS, TM, LANES), _F32),
            pltpu.VMEM((N_PAIRS, TM, LANES), _BF16),
            pltpu.VMEM((2, N_KV_HEADS, band_rows, LANES), _BF16),
            pltpu.VMEM((2, N_KV_HEADS, band_rows, LANES), _BF16),
            pltpu.VMEM((2, KV_W, band_rows), _BF16),
        ],
        compiler_params=pltpu.CompilerParams(
            dimension_semantics=("arbitrary",), vmem_limit_bytes=VMEM_LIMIT),
        name="attn_layer",
    )(sinks, h, h, norm_pre, norm_post, w_in, w_out)


SEG = TM // SUBLANES
CONV_TAIL = (CONV_W - 1) * SUBLANES


def _sublane_scan(a, b, row):
    for d in (1, 2, 4):
        keep = row >= d
        a_sh = jnp.where(keep, pltpu.roll(a, d, 0), 1.0)
        b_sh = jnp.where(keep, pltpu.roll(b, d, 0), 0.0)
        b = a * b_sh + b
        a = a * a_sh
    return a, b


def _lru_kernel(h_ref, hprev_ref, gpre_ref, gpost_ref, win_ref, cw_ref, cb_ref, wa_ref, ba_ref,
                wx_ref, bx_ref, lam_ref, wout_ref, out_ref,
                u_ref, x_ref, tail_ref, y_ref, ynat_ref, yo_ref, sp_ref, hc_ref, perm_ref,
                unperm_ref):
    step = pl.program_id(0)
    cur = step % 2
    prv = 1 - cur

    @pl.when(step == 0)
    def _():
        tail_ref[...] = jnp.zeros(tail_ref.shape, _F32)
        hc_ref[...] = jnp.zeros(hc_ref.shape, _F32)
        nl = -lam_ref[...]
        softplus = jnp.maximum(nl, 0.0) + jnp.log1p(jnp.exp(-jnp.abs(nl)))
        sp_ref[...] = jnp.broadcast_to(softplus, sp_ref.shape)
        ri = lax.broadcasted_iota(jnp.int32, (TM, TM), 0)
        ci = lax.broadcasted_iota(jnp.int32, (TM, TM), 1)
        perm_ref[...] = (ci == (ri % SUBLANES) * SEG + ri // SUBLANES).astype(_BF16)
        unperm_ref[...] = (ci == (ri % SEG) * SUBLANES + ri // SEG).astype(_BF16)

    def block_cols(blk):
        return slice(blk * LRU_BLOCK_W, (blk + 1) * LRU_BLOCK_W)

    def project_x(blk):
        cs = block_cols(blk)
        x_ref[:, cs] = _dot(u_ref[...], win_ref[:, cs])

    def out_project(blk):
        cs = block_cols(blk)
        yo_ref[:, cs] = _dot(ynat_ref[...], wout_ref[:, cs])

    def recur_block(blk, row, first_segment):
        cs = block_cols(blk)
        gate = _dot(u_ref[...],
                    win_ref[:, D_MODEL + blk * LRU_BLOCK_W:D_MODEL + (blk + 1) * LRU_BLOCK_W])

        x = x_ref[:, cs]
        tail_cur = x[TM - CONV_TAIL:]
        tail_prev = tail_ref[:, cs]
        tail_ref[:, cs] = tail_cur
        wrap = []
        for i in range(CONV_W - 1):
            rows = slice(i * SUBLANES, (i + 1) * SUBLANES)
            wrap.append(jnp.where(first_segment, pltpu.roll(tail_prev[rows], 1, 0),
                                  pltpu.roll(tail_cur[rows], 1, 0)))
        xc = cb_ref[0:1, cs] + cw_ref[CONV_W - 1:CONV_W, cs] * x
        for shift in range(1, CONV_W):
            tap = CONV_W - 1 - shift
            shifted = jnp.concatenate(wrap[CONV_W - 1 - shift:] + [x[:TM - shift * SUBLANES]], axis=0)
            xc = xc + cw_ref[tap:tap + 1, cs] * shifted
        xcb = xc.astype(_BF16)
        r = jax.nn.sigmoid(_dot(xcb, wa_ref[blk]) + ba_ref[0:1, cs])
        ig = jax.nn.sigmoid(_dot(xcb, wx_ref[blk]) + bx_ref[0:1, cs])
        sg = gate * jax.nn.sigmoid(gate)

        nla = (C_RG * r) * sp_ref[0:1, cs]
        a = jnp.exp(-nla)
        z = jnp.tanh(nla) * (a * a + 1.0)
        mult = jnp.where(z > 0.0, z * lax.rsqrt(z), 0.0)
        b = mult * (ig * xc)

        acc_a = a[0:SUBLANES]
        acc_h = b[0:SUBLANES]
        prods, local = [acc_a], [acc_h]
        for g in range(1, SEG):
            rows = slice(g * SUBLANES, (g + 1) * SUBLANES)
            acc_h = a[rows] * acc_h + b[rows]
            acc_a = a[rows] * acc_a
            prods.append(acc_a)
            local.append(acc_h)
        carry_in = hc_ref[:, cs]
        seg_a, seg_h = _sublane_scan(acc_a, acc_h, row)
        seg_end = seg_h + seg_a * carry_in
        entering = jnp.where(first_segment, carry_in, pltpu.roll(seg_end, 1, 0))
        hc_ref[:, cs] = jnp.broadcast_to(seg_end[SUBLANES - 1:SUBLANES, :], (SUBLANES, LRU_BLOCK_W))
        ys = [(local[g] + prods[g] * entering) * sg[g * SUBLANES:(g + 1) * SUBLANES]
              for g in range(SEG)]
        y_ref[cur, :, cs] = jnp.concatenate(ys, axis=0).astype(_BF16)

    def body(recur, finish):
        if finish:
            ynat_ref[...] = _dot(unperm_ref[...], y_ref[prv]).astype(_BF16)
            out_project(0)
            out_project(1)
        if recur:
            row = lax.broadcasted_iota(jnp.int32, (SUBLANES, LRU_BLOCK_W), 0)
            first_segment = row == 0
            u_nat = _rms(h_ref[...], gpre_ref[...]).astype(_BF16)
            u_ref[...] = _dot(perm_ref[...], u_nat).astype(_BF16)
            project_x(0)
        for blk in range(N_LRU_BLOCKS):
            if recur and blk + 1 < N_LRU_BLOCKS:
                project_x(blk + 1)
            if finish and blk + 2 < N_LRU_BLOCKS:
                out_project(blk + 2)
            if recur:
                recur_block(blk, row, first_segment)
        if finish:
            out_ref[...] = hprev_ref[...] + _rms(yo_ref[...], gpost_ref[...])

    _staggered(step, body)


def _lru_layer(h, layer, j, norm_pre, norm_post, *params):
    lagged = lambda i: (jnp.maximum(i - 1, 0), 0)
    return pl.pallas_call(
        _lru_kernel,
        out_shape=jax.ShapeDtypeStruct((SEQ, D_MODEL), _F32),
        grid=(N_TILES + 1,),
        in_specs=[
            pl.BlockSpec((TM, D_MODEL), lambda i: (jnp.minimum(i, N_TILES - 1), 0)),
            pl.BlockSpec((TM, D_MODEL), lagged),
            _layer_spec(norm_pre, layer),
            _layer_spec(norm_post, layer),
        ] + [_layer_spec(p, j) for p in params],
        out_specs=pl.BlockSpec((TM, D_MODEL), lagged),
        scratch_shapes=[
            pltpu.VMEM((TM, D_MODEL), _BF16),
            pltpu.VMEM((TM, D_MODEL), _F32),
            pltpu.VMEM((CONV_TAIL, D_MODEL), _F32),
            pltpu.VMEM((2, TM, D_MODEL), _BF16),
            pltpu.VMEM((TM, D_MODEL), _BF16),
            pltpu.VMEM((TM, D_MODEL), _F32),
            pltpu.VMEM((SUBLANES, D_MODEL), _F32),
            pltpu.VMEM((SUBLANES, D_MODEL), _F32),
            pltpu.VMEM((TM, TM), _BF16),
            pltpu.VMEM((TM, TM), _BF16),
        ],
        compiler_params=pltpu.CompilerParams(
            dimension_semantics=("arbitrary",), vmem_limit_bytes=VMEM_LIMIT),
        name="lru_layer",
    )(h, h, norm_pre, norm_post, *params)


def kernel(x, norm_pre, norm_post, attn_w_in, attn_w_out, attn_sinks, lru_w_in, lru_conv_w,
           lru_conv_b, lru_w_a, lru_b_a, lru_w_x, lru_b_x, lru_lambda, lru_w_out):
    depth = norm_pre.shape[0]
    row = lambda v: v.reshape(v.shape[0], 1, D_MODEL)
    norm_pre, norm_post = row(norm_pre), row(norm_post)
    attn_params = (attn_w_in.astype(_BF16), attn_w_out.astype(_BF16), attn_sinks)
    lru_params = (lru_w_in.astype(_BF16), lru_conv_w, row(lru_conv_b),
                  lru_w_a.astype(_BF16), row(lru_b_a), lru_w_x.astype(_BF16), row(lru_b_x),
                  row(lru_lambda), lru_w_out.astype(_BF16))
    h = x.reshape(SEQ, D_MODEL)
    for layer in range(depth):
        j = layer // 2
        if layer % 2 == 0:
            h = _attn_layer(h, layer, j, norm_pre, norm_post, *attn_params)
        else:
            h = _lru_layer(h, layer, j, norm_pre, norm_post, *lru_params)
    return h.reshape(x.shape)
```

```python
import functools

import jax
import jax.numpy as jnp
from jax import lax
from jax.experimental import pallas as pl
from jax.experimental.pallas import tpu as pltpu

D_MODEL = 2048
SEQ = 8192
HEAD_DIM = 64
N_HEADS = 32
N_KV_HEADS = 4
GROUP = N_HEADS // N_KV_HEADS
BLOCK = 128
KV_W = N_KV_HEADS * HEAD_DIM
GATE_COL = D_MODEL + 2 * KV_W
LRU_BLOCK_W = 256
N_LRU_BLOCKS = D_MODEL // LRU_BLOCK_W
CONV_W = 4
C_RG = 8.0
NORM_EPS = 1e-6
MASK_VALUE = -1e30

LANES = 128
SUBLANES = 8
N_PAIRS = N_HEADS // 2
PAIRS_PER_KV = GROUP // 2

TM = 256
N_TILES = SEQ // TM
VMEM_LIMIT = 60 * 1024 * 1024

_F32 = jnp.float32
_BF16 = jnp.bfloat16


def _rms(x, g):
    ms = jnp.mean(x * x, axis=-1, keepdims=True)
    return x * lax.rsqrt(ms + NORM_EPS) * g


def _dot(a, b):
    return jnp.dot(a, b, preferred_element_type=_F32)


def _dot_nt(a, b):
    return lax.dot_general(a, b, (((1,), (1,)), ((), ())), preferred_element_type=_F32)


CAST_ROWS = D_MODEL // N_TILES


def _cast_rows(srcs, dsts):
    for src, dst in zip(srcs, dsts):
        dst[...] = src[...].astype(_BF16)


def _cast_specs(stacked, j):
    rows = lambda i: jnp.minimum(i, N_TILES - 1)
    in_specs = [pl.BlockSpec((None, CAST_ROWS, w.shape[-1]), lambda i: (j, rows(i), 0)) for w in stacked]
    out_specs = [pl.BlockSpec((CAST_ROWS, w.shape[-1]), lambda i: (rows(i), 0)) for w in stacked]
    out_shapes = [jax.ShapeDtypeStruct((D_MODEL, w.shape[-1]), _BF16) for w in stacked]
    return in_specs, out_specs, out_shapes


def _resident(arr):
    zeros = (0,) * arr.ndim
    return pl.BlockSpec(arr.shape, lambda i: zeros, pipeline_mode=pl.Buffered(1))


def _staggered(step, body):
    pl.when(step == 0)(lambda: body(True, False))
    pl.when(jnp.logical_and(step > 0, step < N_TILES))(lambda: body(True, True))
    pl.when(step == N_TILES)(lambda: body(False, True))


def _attn_kernel(j, n_cast, sink_ref, h_ref, hprev_ref, gpre_ref, gpost_ref, win_ref, wout_ref,
                 *refs):
    cast_src, (out_ref, *cast_dst), scratch = refs[:n_cast], refs[n_cast:2 * n_cast + 1], refs[2 * n_cast + 1:]
    u_ref, q3_ref, sg3_ref, y3_ref, klo_ref, khi_ref, vt_ref = scratch
    _cast_rows(cast_src, cast_dst)
    step = pl.program_id(0)
    cur = step % 2
    prv = 1 - cur
    nb = TM // BLOCK
    stack = PAIRS_PER_KV * BLOCK
    chunks = BLOCK // SUBLANES
    per_kv = GROUP * HEAD_DIM

    def project_kv(first):
        kv = _dot(u_ref[...], win_ref[:, D_MODEL:GATE_COL])
        lane_lo = lax.broadcasted_iota(jnp.int32, (TM, LANES), 1) < HEAD_DIM
        new = slice(BLOCK, BLOCK + TM)
        for c in range(N_KV_HEADS // 2):
            orig = kv[:, c * LANES:(c + 1) * LANES]
            sw = pltpu.roll(orig, HEAD_DIM, 1)
            klo_ref[cur, 2 * c, new, :] = jnp.where(lane_lo, orig, 0.0).astype(_BF16)
            khi_ref[cur, 2 * c, new, :] = jnp.where(lane_lo, 0.0, sw).astype(_BF16)
            klo_ref[cur, 2 * c + 1, new, :] = jnp.where(lane_lo, sw, 0.0).astype(_BF16)
            khi_ref[cur, 2 * c + 1, new, :] = jnp.where(lane_lo, 0.0, orig).astype(_BF16)
        vt_ref[cur, :, new] = kv[:, KV_W:2 * KV_W].T.astype(_BF16)
        for ref in (klo_ref, khi_ref):
            prev = jnp.zeros((N_KV_HEADS, BLOCK, LANES), _BF16) if first else ref[prv, :, TM:TM + BLOCK, :]
            ref[cur, :, 0:BLOCK, :] = prev
        vt_ref[cur, :, 0:BLOCK] = (jnp.zeros((KV_W, BLOCK), _BF16) if first
                                   else vt_ref[prv, :, TM:TM + BLOCK])

    def reduce_rows(x3, op):
        r = x3[0]
        for c in range(1, chunks):
            r = op(r, x3[c])
        for d in (1, 2, 4):
            r = op(r, pltpu.roll(r, d, 0))
        return r

    def softmax_band(s_full, sink_row, b, tri, prev_exists):
        s_prev = s_full[0:BLOCK]
        s_cur = s_full[BLOCK:2 * BLOCK]
        if b == 0:
            s_prev = jnp.where(prev_exists, s_prev, MASK_VALUE)
        s3 = jnp.where(tri, s_cur, s_prev).reshape(chunks, SUBLANES, stack)
        m = jnp.maximum(reduce_rows(s3, jnp.maximum), sink_row)
        e3 = jnp.exp(s3 - m[None])
        den = reduce_rows(e3, jnp.add) + jnp.exp(sink_row - m)
        pn = (e3 * (1.0 / den)[None]).reshape(BLOCK, stack)
        band = jnp.concatenate([jnp.where(tri, 0.0, pn), jnp.where(tri, pn, 0.0)], axis=0)
        return band.astype(_BF16)

    def sink_row(kvh, odd):
        cols = [jnp.full((SUBLANES, BLOCK), sink_ref[j, 2 * (PAIRS_PER_KV * kvh + i) + odd], _F32)
                for i in range(PAIRS_PER_KV)]
        return jnp.concatenate(cols, axis=1)

    def body(project, attend):
        if project:
            u_ref[...] = _rms(h_ref[...], gpre_ref[...]).astype(_BF16)
        if attend:
            kj = lax.broadcasted_iota(jnp.int32, (BLOCK, stack), 0)
            qi = lax.broadcasted_iota(jnp.int32, (BLOCK, stack), 1) & (BLOCK - 1)
            tri = kj <= qi
            prev_exists = (kj + ((step - 1) * TM - BLOCK)) >= 0

        for kvh in range(N_KV_HEADS):
            pairs = slice(PAIRS_PER_KV * kvh, PAIRS_PER_KV * (kvh + 1))
            if attend:
                scores = []
                for b in range(nb):
                    rows = slice(b * BLOCK, (b + 1) * BLOCK)
                    band = slice(b * BLOCK, (b + 2) * BLOCK)
                    qs = q3_ref[prv, pairs, rows, :].reshape(stack, LANES)
                    scores.append((_dot_nt(klo_ref[prv, kvh, band, :], qs),
                                   _dot_nt(khi_ref[prv, kvh, band, :], qs)))

            if project:
                q = _dot(u_ref[...], win_ref[:, kvh * per_kv:(kvh + 1) * per_kv])
                gate = _dot(u_ref[...],
                            win_ref[:, GATE_COL + kvh * per_kv:GATE_COL + (kvh + 1) * per_kv])
                for i in range(PAIRS_PER_KV):
                    p = PAIRS_PER_KV * kvh + i
                    q3_ref[cur, p] = (q[:, i * LANES:(i + 1) * LANES] * (1.0 / 8.0)).astype(_BF16)
                    g = gate[:, i * LANES:(i + 1) * LANES]
                    sg3_ref[cur, p] = g * jax.nn.sigmoid(g)

            if attend:
                sink_e = sink_row(kvh, 0)
                sink_o = sink_row(kvh, 1)
                for b in range(nb):
                    rows = slice(b * BLOCK, (b + 1) * BLOCK)
                    band = slice(b * BLOCK, (b + 2) * BLOCK)
                    vt = vt_ref[prv, kvh * HEAD_DIM:(kvh + 1) * HEAD_DIM, band]
                    o_e = _dot(vt, softmax_band(scores[b][0], sink_e, b, tri, prev_exists))
                    o_o = _dot(vt, softmax_band(scores[b][1], sink_o, b, tri, prev_exists))
                    o2 = jnp.concatenate([o_e, o_o], axis=0).T
                    sg = sg3_ref[prv, pairs, rows, :].reshape(stack, LANES)
                    y3_ref[pairs, rows, :] = (o2 * sg).astype(_BF16).reshape(PAIRS_PER_KV, BLOCK, LANES)

        if project:
            project_kv(first=not attend)

        if attend:
            y = jnp.concatenate([y3_ref[p] for p in range(N_PAIRS)], axis=1)
            yo = _dot(y, wout_ref[...])
            out_ref[...] = hprev_ref[...] + _rms(yo, gpost_ref[...])

    _staggered(step, body)


def _layer_spec(stacked, j):
    zeros = (0,) * (stacked.ndim - 1)
    return pl.BlockSpec((None,) + stacked.shape[1:], lambda i: (j,) + zeros,
                        pipeline_mode=pl.Buffered(1))


def _attn_layer(h, layer, j, norm_pre, norm_post, sinks, w_in, w_out, next_weights, j_next):
    band_rows = BLOCK + TM
    lagged = lambda i: (jnp.maximum(i - 1, 0), 0)
    cast_in, cast_out, cast_shapes = _cast_specs(next_weights, j_next)
    return pl.pallas_call(
        functools.partial(_attn_kernel, j, len(next_weights)),
        out_shape=[jax.ShapeDtypeStruct((SEQ, D_MODEL), _F32)] + cast_shapes,
        grid=(N_TILES + 1,),
        in_specs=[
            pl.BlockSpec(memory_space=pltpu.SMEM),
            pl.BlockSpec((TM, D_MODEL), lambda i: (jnp.minimum(i, N_TILES - 1), 0)),
            pl.BlockSpec((TM, D_MODEL), lagged),
            _layer_spec(norm_pre, layer),
            _layer_spec(norm_post, layer),
            _resident(w_in),
            _resident(w_out),
        ] + cast_in,
        out_specs=[pl.BlockSpec((TM, D_MODEL), lagged)] + cast_out,
        scratch_shapes=[
            pltpu.VMEM((TM, D_MODEL), _BF16),
            pltpu.VMEM((2, N_PAIRS, TM, LANES), _BF16),
            pltpu.VMEM((2, N_PAIRS, TM, LANES), _F32),
            pltpu.VMEM((N_PAIRS, TM, LANES), _BF16),
            pltpu.VMEM((2, N_KV_HEADS, band_rows, LANES), _BF16),
            pltpu.VMEM((2, N_KV_HEADS, band_rows, LANES), _BF16),
            pltpu.VMEM((2, KV_W, band_rows), _BF16),
        ],
        compiler_params=pltpu.CompilerParams(
            dimension_semantics=("arbitrary",), vmem_limit_bytes=VMEM_LIMIT),
        name="attn_layer",
    )(sinks, h, h, norm_pre, norm_post, w_in, w_out, *next_weights)


SEG = TM // SUBLANES
CONV_TAIL = (CONV_W - 1) * SUBLANES


def _sublane_scan(a, b, row):
    for d in (1, 2, 4):
        keep = row >= d
        a_sh = jnp.where(keep, pltpu.roll(a, d, 0), 1.0)
        b_sh = jnp.where(keep, pltpu.roll(b, d, 0), 0.0)
        b = a * b_sh + b
        a = a * a_sh
    return a, b


def _lru_kernel(n_cast, h_ref, hprev_ref, gpre_ref, gpost_ref, win_ref, cw_ref, cb_ref, wa_ref,
                ba_ref, wx_ref, bx_ref, lam_ref, wout_ref, *refs):
    cast_src, (out_ref, *cast_dst), scratch = refs[:n_cast], refs[n_cast:2 * n_cast + 1], refs[2 * n_cast + 1:]
    u_ref, x_ref, tail_ref, y_ref, ynat_ref, yo_ref, sp_ref, hc_ref, perm_ref, unperm_ref = scratch
    _cast_rows(cast_src, cast_dst)
    step = pl.program_id(0)
    cur = step % 2
    prv = 1 - cur

    @pl.when(step == 0)
    def _():
        tail_ref[...] = jnp.zeros(tail_ref.shape, _F32)
        hc_ref[...] = jnp.zeros(hc_ref.shape, _F32)
        nl = -lam_ref[...]
        softplus = jnp.maximum(nl, 0.0) + jnp.log1p(jnp.exp(-jnp.abs(nl)))
        sp_ref[...] = jnp.broadcast_to(softplus, sp_ref.shape)
        ri = lax.broadcasted_iota(jnp.int32, (TM, TM), 0)
        ci = lax.broadcasted_iota(jnp.int32, (TM, TM), 1)
        perm_ref[...] = (ci == (ri % SUBLANES) * SEG + ri // SUBLANES).astype(_BF16)
        unperm_ref[...] = (ci == (ri % SEG) * SUBLANES + ri // SEG).astype(_BF16)

    def block_cols(blk):
        return slice(blk * LRU_BLOCK_W, (blk + 1) * LRU_BLOCK_W)

    def project_x(blk):
        cs = block_cols(blk)
        x_ref[:, cs] = _dot(u_ref[...], win_ref[:, cs])

    def out_project(blk):
        cs = block_cols(blk)
        yo_ref[:, cs] = _dot(ynat_ref[...], wout_ref[:, cs])

    def recur_block(blk, row, first_segment):
        cs = block_cols(blk)
        gate = _dot(u_ref[...],
                    win_ref[:, D_MODEL + blk * LRU_BLOCK_W:D_MODEL + (blk + 1) * LRU_BLOCK_W])

        x = x_ref[:, cs]
        tail_cur = x[TM - CONV_TAIL:]
        tail_prev = tail_ref[:, cs]
        tail_ref[:, cs] = tail_cur
        wrap = []
        for i in range(CONV_W - 1):
            rows = slice(i * SUBLANES, (i + 1) * SUBLANES)
            wrap.append(jnp.where(first_segment, pltpu.roll(tail_prev[rows], 1, 0),
                                  pltpu.roll(tail_cur[rows], 1, 0)))
        xc = cb_ref[0:1, cs] + cw_ref[CONV_W - 1:CONV_W, cs] * x
        for shift in range(1, CONV_W):
            tap = CONV_W - 1 - shift
            shifted = jnp.concatenate(wrap[CONV_W - 1 - shift:] + [x[:TM - shift * SUBLANES]], axis=0)
            xc = xc + cw_ref[tap:tap + 1, cs] * shifted
        xcb = xc.astype(_BF16)
        r = jax.nn.sigmoid(_dot(xcb, wa_ref[blk]) + ba_ref[0:1, cs])
        ig = jax.nn.sigmoid(_dot(xcb, wx_ref[blk]) + bx_ref[0:1, cs])
        sg = gate * jax.nn.sigmoid(gate)

        nla = (C_RG * r) * sp_ref[0:1, cs]
        a = jnp.exp(-nla)
        z = jnp.tanh(nla) * (a * a + 1.0)
        mult = jnp.where(z > 0.0, z * lax.rsqrt(z), 0.0)
        b = mult * (ig * xc)

        acc_a = a[0:SUBLANES]
        acc_h = b[0:SUBLANES]
        prods, local = [acc_a], [acc_h]
        for g in range(1, SEG):
            rows = slice(g * SUBLANES, (g + 1) * SUBLANES)
            acc_h = a[rows] * acc_h + b[rows]
            acc_a = a[rows] * acc_a
            prods.append(acc_a)
            local.append(acc_h)
        carry_in = hc_ref[:, cs]
        seg_a, seg_h = _sublane_scan(acc_a, acc_h, row)
        seg_end = seg_h + seg_a * carry_in
        entering = jnp.where(first_segment, carry_in, pltpu.roll(seg_end, 1, 0))
        hc_ref[:, cs] = jnp.broadcast_to(seg_end[SUBLANES - 1:SUBLANES, :], (SUBLANES, LRU_BLOCK_W))
        ys = [(local[g] + prods[g] * entering) * sg[g * SUBLANES:(g + 1) * SUBLANES]
              for g in range(SEG)]
        y_ref[cur, :, cs] = jnp.concatenate(ys, axis=0).astype(_BF16)

    def body(recur, finish):
        if finish:
            ynat_ref[...] = _dot(unperm_ref[...], y_ref[prv]).astype(_BF16)
            out_project(0)
            out_project(1)
        if recur:
            row = lax.broadcasted_iota(jnp.int32, (SUBLANES, LRU_BLOCK_W), 0)
            first_segment = row == 0
            u_nat = _rms(h_ref[...], gpre_ref[...]).astype(_BF16)
            u_ref[...] = _dot(perm_ref[...], u_nat).astype(_BF16)
            project_x(0)
        for blk in range(N_LRU_BLOCKS):
            if recur and blk + 1 < N_LRU_BLOCKS:
                project_x(blk + 1)
            if finish and blk + 2 < N_LRU_BLOCKS:
                out_project(blk + 2)
            if recur:
                recur_block(blk, row, first_segment)
        if finish:
            out_ref[...] = hprev_ref[...] + _rms(yo_ref[...], gpost_ref[...])

    _staggered(step, body)


def _lru_layer(h, layer, j, norm_pre, norm_post, small, weights, next_weights, j_next):
    lagged = lambda i: (jnp.maximum(i - 1, 0), 0)
    conv_w, conv_b, b_a, b_x, lam = small
    w_in, w_a, w_x, w_out = weights
    gate_shape = (N_LRU_BLOCKS, LRU_BLOCK_W, LRU_BLOCK_W)
    w_a, w_x = w_a.reshape(gate_shape), w_x.reshape(gate_shape)
    cast_in, cast_out, cast_shapes = _cast_specs(next_weights, j_next)
    return pl.pallas_call(
        functools.partial(_lru_kernel, len(next_weights)),
        out_shape=[jax.ShapeDtypeStruct((SEQ, D_MODEL), _F32)] + cast_shapes,
        grid=(N_TILES + 1,),
        in_specs=[
            pl.BlockSpec((TM, D_MODEL), lambda i: (jnp.minimum(i, N_TILES - 1), 0)),
            pl.BlockSpec((TM, D_MODEL), lagged),
            _layer_spec(norm_pre, layer),
            _layer_spec(norm_post, layer),
            _resident(w_in), _layer_spec(conv_w, j), _layer_spec(conv_b, j),
            _resident(w_a), _layer_spec(b_a, j), _resident(w_x), _layer_spec(b_x, j),
            _layer_spec(lam, j), _resident(w_out),
        ] + cast_in,
        out_specs=[pl.BlockSpec((TM, D_MODEL), lagged)] + cast_out,
        scratch_shapes=[
            pltpu.VMEM((TM, D_MODEL), _BF16),
            pltpu.VMEM((TM, D_MODEL), _F32),
            pltpu.VMEM((CONV_TAIL, D_MODEL), _F32),
            pltpu.VMEM((2, TM, D_MODEL), _BF16),
            pltpu.VMEM((TM, D_MODEL), _BF16),
            pltpu.VMEM((TM, D_MODEL), _F32),
            pltpu.VMEM((SUBLANES, D_MODEL), _F32),
            pltpu.VMEM((SUBLANES, D_MODEL), _F32),
            pltpu.VMEM((TM, TM), _BF16),
            pltpu.VMEM((TM, TM), _BF16),
        ],
        compiler_params=pltpu.CompilerParams(
            dimension_semantics=("arbitrary",), vmem_limit_bytes=VMEM_LIMIT),
        name="lru_layer",
    )(h, h, norm_pre, norm_post, w_in, conv_w, conv_b, w_a, b_a, w_x, b_x, lam, w_out, *next_weights)


def kernel(x, norm_pre, norm_post, attn_w_in, attn_w_out, attn_sinks, lru_w_in, lru_conv_w,
           lru_conv_b, lru_w_a, lru_b_a, lru_w_x, lru_b_x, lru_lambda, lru_w_out):
    depth = norm_pre.shape[0]
    row = lambda v: v.reshape(v.shape[0], 1, D_MODEL)
    norm_pre, norm_post = row(norm_pre), row(norm_post)
    lru_small = (lru_conv_w, row(lru_conv_b), row(lru_b_a), row(lru_b_x), row(lru_lambda))
    rows = lambda w: w.reshape(w.shape[0], D_MODEL, -1)
    attn_f32 = (attn_w_in, attn_w_out)
    lru_f32 = (lru_w_in, rows(lru_w_a), rows(lru_w_x), lru_w_out)
    weights = tuple(w[0].astype(_BF16) for w in attn_f32)
    h = x.reshape(SEQ, D_MODEL)
    for layer in range(depth):
        j = layer // 2
        last = layer == depth - 1
        if layer % 2 == 0:
            h, *weights = _attn_layer(h, layer, j, norm_pre, norm_post, attn_sinks, *weights,
                                      () if last else lru_f32, j)
        else:
            h, *weights = _lru_layer(h, layer, j, norm_pre, norm_post, lru_small, weights,
                                     () if last else attn_f32, j + 1)
    return h.reshape(x.shape)
```

```python
import functools

import jax
import jax.numpy as jnp
from jax import lax
from jax.experimental import pallas as pl
from jax.experimental.pallas import tpu as pltpu

D_MODEL = 2048
SEQ = 8192
HEAD_DIM = 64
N_HEADS = 32
N_KV_HEADS = 4
GROUP = N_HEADS // N_KV_HEADS
BLOCK = 128
KV_W = N_KV_HEADS * HEAD_DIM
GATE_COL = D_MODEL + 2 * KV_W
LRU_BLOCK_W = 256
N_LRU_BLOCKS = D_MODEL // LRU_BLOCK_W
CONV_W = 4
C_RG = 8.0
NORM_EPS = 1e-6
MASK_VALUE = -1e30

LANES = 128
SUBLANES = 8
N_PAIRS = N_HEADS // 2
PAIRS_PER_KV = GROUP // 2

TM = 256
N_TILES = SEQ // TM
VMEM_LIMIT = 60 * 1024 * 1024

_F32 = jnp.float32
_BF16 = jnp.bfloat16


def _rms(x, g):
    ms = jnp.mean(x * x, axis=-1, keepdims=True)
    return x * lax.rsqrt(ms + NORM_EPS) * g


def _dot(a, b):
    return jnp.dot(a, b, preferred_element_type=_F32)


def _dot_nt(a, b):
    return lax.dot_general(a, b, (((1,), (1,)), ((), ())), preferred_element_type=_F32)


CAST_ROWS = D_MODEL // N_TILES


def _cast_rows(srcs, dsts):
    for src, dst in zip(srcs, dsts):
        dst[...] = src[...].astype(_BF16)


def _cast_specs(stacked, j):
    rows = lambda i: jnp.minimum(i, N_TILES - 1)
    in_specs = [pl.BlockSpec((None, CAST_ROWS, w.shape[-1]), lambda i: (j, rows(i), 0)) for w in stacked]
    out_specs = [pl.BlockSpec((CAST_ROWS, w.shape[-1]), lambda i: (rows(i), 0)) for w in stacked]
    out_shapes = [jax.ShapeDtypeStruct((D_MODEL, w.shape[-1]), _BF16) for w in stacked]
    return in_specs, out_specs, out_shapes


def _resident(arr):
    zeros = (0,) * arr.ndim
    return pl.BlockSpec(arr.shape, lambda i: zeros, pipeline_mode=pl.Buffered(1))


def _staggered(step, body):
    pl.when(step == 0)(lambda: body(True, False))
    pl.when(jnp.logical_and(step > 0, step < N_TILES))(lambda: body(True, True))
    pl.when(step == N_TILES)(lambda: body(False, True))


def _attn_kernel(layer, j, n_cast, sink_ref, h_ref, hprev_ref, gpre_ref, gpost_ref, win_ref, wout_ref,
                 *refs):
    cast_src, (out_ref, *cast_dst), scratch = refs[:n_cast], refs[n_cast:2 * n_cast + 1], refs[2 * n_cast + 1:]
    u_ref, q3_ref, sg3_ref, y3_ref, klo_ref, khi_ref, vt_ref = scratch
    _cast_rows(cast_src, cast_dst)
    step = pl.program_id(0)
    cur = step % 2
    prv = 1 - cur
    nb = TM // BLOCK
    stack = PAIRS_PER_KV * BLOCK
    chunks = BLOCK // SUBLANES
    per_kv = GROUP * HEAD_DIM

    def project_kv(first):
        kv = _dot(u_ref[...], win_ref[:, D_MODEL:GATE_COL])
        lane_lo = lax.broadcasted_iota(jnp.int32, (TM, LANES), 1) < HEAD_DIM
        new = slice(BLOCK, BLOCK + TM)
        for c in range(N_KV_HEADS // 2):
            orig = kv[:, c * LANES:(c + 1) * LANES]
            sw = pltpu.roll(orig, HEAD_DIM, 1)
            klo_ref[cur, 2 * c, new, :] = jnp.where(lane_lo, orig, 0.0).astype(_BF16)
            khi_ref[cur, 2 * c, new, :] = jnp.where(lane_lo, 0.0, sw).astype(_BF16)
            klo_ref[cur, 2 * c + 1, new, :] = jnp.where(lane_lo, sw, 0.0).astype(_BF16)
            khi_ref[cur, 2 * c + 1, new, :] = jnp.where(lane_lo, 0.0, orig).astype(_BF16)
        vt_ref[cur, :, new] = kv[:, KV_W:2 * KV_W].T.astype(_BF16)
        for ref in (klo_ref, khi_ref):
            prev = jnp.zeros((N_KV_HEADS, BLOCK, LANES), _BF16) if first else ref[prv, :, TM:TM + BLOCK, :]
            ref[cur, :, 0:BLOCK, :] = prev
        vt_ref[cur, :, 0:BLOCK] = (jnp.zeros((KV_W, BLOCK), _BF16) if first
                                   else vt_ref[prv, :, TM:TM + BLOCK])

    def reduce_rows(x3, op):
        r = x3[0]
        for c in range(1, chunks):
            r = op(r, x3[c])
        for d in (1, 2, 4):
            r = op(r, pltpu.roll(r, d, 0))
        return r

    def softmax_band(s_full, sink_row, b, tri, prev_exists):
        s_prev = s_full[0:BLOCK]
        s_cur = s_full[BLOCK:2 * BLOCK]
        if b == 0:
            s_prev = jnp.where(prev_exists, s_prev, MASK_VALUE)
        s3 = jnp.where(tri, s_cur, s_prev).reshape(chunks, SUBLANES, stack)
        m = jnp.maximum(reduce_rows(s3, jnp.maximum), sink_row)
        e3 = jnp.exp(s3 - m[None])
        den = reduce_rows(e3, jnp.add) + jnp.exp(sink_row - m)
        pn = (e3 * (1.0 / den)[None]).reshape(BLOCK, stack)
        band = jnp.concatenate([jnp.where(tri, 0.0, pn), jnp.where(tri, pn, 0.0)], axis=0)
        return band.astype(_BF16)

    def sink_row(kvh, odd):
        cols = [jnp.full((SUBLANES, BLOCK), sink_ref[j, 2 * (PAIRS_PER_KV * kvh + i) + odd], _F32)
                for i in range(PAIRS_PER_KV)]
        return jnp.concatenate(cols, axis=1)

    def body(project, attend):
        if project:
            u_ref[...] = _rms(h_ref[...], gpre_ref[layer:layer + 1, :]).astype(_BF16)
        if attend:
            kj = lax.broadcasted_iota(jnp.int32, (BLOCK, stack), 0)
            qi = lax.broadcasted_iota(jnp.int32, (BLOCK, stack), 1) & (BLOCK - 1)
            tri = kj <= qi
            prev_exists = (kj + ((step - 1) * TM - BLOCK)) >= 0

        for kvh in range(N_KV_HEADS):
            pairs = slice(PAIRS_PER_KV * kvh, PAIRS_PER_KV * (kvh + 1))
            if attend:
                scores = []
                for b in range(nb):
                    rows = slice(b * BLOCK, (b + 1) * BLOCK)
                    band = slice(b * BLOCK, (b + 2) * BLOCK)
                    qs = q3_ref[prv, pairs, rows, :].reshape(stack, LANES)
                    scores.append((_dot_nt(klo_ref[prv, kvh, band, :], qs),
                                   _dot_nt(khi_ref[prv, kvh, band, :], qs)))

            if project:
                q = _dot(u_ref[...], win_ref[:, kvh * per_kv:(kvh + 1) * per_kv])
                gate = _dot(u_ref[...],
                            win_ref[:, GATE_COL + kvh * per_kv:GATE_COL + (kvh + 1) * per_kv])
                for i in range(PAIRS_PER_KV):
                    p = PAIRS_PER_KV * kvh + i
                    q3_ref[cur, p] = (q[:, i * LANES:(i + 1) * LANES] * (1.0 / 8.0)).astype(_BF16)
                    g = gate[:, i * LANES:(i + 1) * LANES]
                    sg3_ref[cur, p] = g * jax.nn.sigmoid(g)

            if attend:
                sink_e = sink_row(kvh, 0)
                sink_o = sink_row(kvh, 1)
                for b in range(nb):
                    rows = slice(b * BLOCK, (b + 1) * BLOCK)
                    band = slice(b * BLOCK, (b + 2) * BLOCK)
                    vt = vt_ref[prv, kvh * HEAD_DIM:(kvh + 1) * HEAD_DIM, band]
                    o_e = _dot(vt, softmax_band(scores[b][0], sink_e, b, tri, prev_exists))
                    o_o = _dot(vt, softmax_band(scores[b][1], sink_o, b, tri, prev_exists))
                    o2 = jnp.concatenate([o_e, o_o], axis=0).T
                    sg = sg3_ref[prv, pairs, rows, :].reshape(stack, LANES)
                    y3_ref[pairs, rows, :] = (o2 * sg).astype(_BF16).reshape(PAIRS_PER_KV, BLOCK, LANES)

        if attend:
            y = jnp.concatenate([y3_ref[p] for p in range(N_PAIRS)], axis=1)
            yo = _dot(y, wout_ref[...])
        if project:
            project_kv(first=not attend)
        if attend:
            out_ref[...] = hprev_ref[...] + _rms(yo, gpost_ref[layer:layer + 1, :])

    _staggered(step, body)


def _layer_spec(stacked, j):
    zeros = (0,) * (stacked.ndim - 1)
    return pl.BlockSpec((None,) + stacked.shape[1:], lambda i: (j,) + zeros,
                        pipeline_mode=pl.Buffered(1))


def _attn_layer(h, layer, j, norm_pre, norm_post, sinks, w_in, w_out, next_weights, j_next):
    band_rows = BLOCK + TM
    lagged = lambda i: (jnp.maximum(i - 1, 0), 0)
    cast_in, cast_out, cast_shapes = _cast_specs(next_weights, j_next)
    return pl.pallas_call(
        functools.partial(_attn_kernel, layer, j, len(next_weights)),
        out_shape=[jax.ShapeDtypeStruct((SEQ, D_MODEL), _F32)] + cast_shapes,
        grid=(N_TILES + 1,),
        in_specs=[
            pl.BlockSpec(memory_space=pltpu.SMEM),
            pl.BlockSpec((TM, D_MODEL), lambda i: (jnp.minimum(i, N_TILES - 1), 0)),
            pl.BlockSpec((TM, D_MODEL), lagged),
            _resident(norm_pre),
            _resident(norm_post),
            _resident(w_in),
            _resident(w_out),
        ] + cast_in,
        out_specs=[pl.BlockSpec((TM, D_MODEL), lagged)] + cast_out,
        scratch_shapes=[
            pltpu.VMEM((TM, D_MODEL), _BF16),
            pltpu.VMEM((2, N_PAIRS, TM, LANES), _BF16),
            pltpu.VMEM((2, N_PAIRS, TM, LANES), _F32),
            pltpu.VMEM((N_PAIRS, TM, LANES), _BF16),
            pltpu.VMEM((2, N_KV_HEADS, band_rows, LANES), _BF16),
            pltpu.VMEM((2, N_KV_HEADS, band_rows, LANES), _BF16),
            pltpu.VMEM((2, KV_W, band_rows), _BF16),
        ],
        compiler_params=pltpu.CompilerParams(
            dimension_semantics=("arbitrary",), vmem_limit_bytes=VMEM_LIMIT),
        name="attn_layer",
    )(sinks, h, h, norm_pre, norm_post, w_in, w_out, *next_weights)


SEG = TM // SUBLANES
CONV_TAIL = (CONV_W - 1) * SUBLANES


def _sublane_scan(a, b, row):
    for d in (1, 2, 4):
        keep = row >= d
        a_sh = jnp.where(keep, pltpu.roll(a, d, 0), 1.0)
        b_sh = jnp.where(keep, pltpu.roll(b, d, 0), 0.0)
        b = a * b_sh + b
        a = a * a_sh
    return a, b


def _lru_kernel(layer, j, n_cast, h_ref, hprev_ref, gpre_ref, gpost_ref, win_ref, cw_ref, cb_ref, wa_ref,
                ba_ref, wx_ref, bx_ref, lam_ref, wout_ref, *refs):
    cast_src, (out_ref, *cast_dst), scratch = refs[:n_cast], refs[n_cast:2 * n_cast + 1], refs[2 * n_cast + 1:]
    u_ref, x_ref, tail_ref, y_ref, ynat_ref, yo_ref, sp_ref, hc_ref, perm_ref, unperm_ref = scratch
    _cast_rows(cast_src, cast_dst)
    step = pl.program_id(0)
    cur = step % 2
    prv = 1 - cur

    @pl.when(step == 0)
    def _():
        tail_ref[...] = jnp.zeros(tail_ref.shape, _F32)
        hc_ref[...] = jnp.zeros(hc_ref.shape, _F32)
        nl = -lam_ref[j:j + 1, :]
        softplus = jnp.maximum(nl, 0.0) + jnp.log1p(jnp.exp(-jnp.abs(nl)))
        sp_ref[...] = jnp.broadcast_to(softplus, sp_ref.shape)
        ri = lax.broadcasted_iota(jnp.int32, (TM, TM), 0)
        ci = lax.broadcasted_iota(jnp.int32, (TM, TM), 1)
        perm_ref[...] = (ci == (ri % SUBLANES) * SEG + ri // SUBLANES).astype(_BF16)
        unperm_ref[...] = (ci == (ri % SEG) * SUBLANES + ri // SEG).astype(_BF16)

    def block_cols(blk):
        return slice(blk * LRU_BLOCK_W, (blk + 1) * LRU_BLOCK_W)

    def project_x(blk):
        cs = block_cols(blk)
        x_ref[:, cs] = _dot(u_ref[...], win_ref[:, cs])

    def out_project(blk):
        cs = block_cols(blk)
        yo_ref[:, cs] = _dot(ynat_ref[...], wout_ref[:, cs])

    def recur_block(blk, row, first_segment):
        cs = block_cols(blk)
        gate = _dot(u_ref[...],
                    win_ref[:, D_MODEL + blk * LRU_BLOCK_W:D_MODEL + (blk + 1) * LRU_BLOCK_W])

        x = x_ref[:, cs]
        tail_cur = x[TM - CONV_TAIL:]
        tail_prev = tail_ref[:, cs]
        tail_ref[:, cs] = tail_cur
        wrap = []
        for i in range(CONV_W - 1):
            rows = slice(i * SUBLANES, (i + 1) * SUBLANES)
            wrap.append(jnp.where(first_segment, pltpu.roll(tail_prev[rows], 1, 0),
                                  pltpu.roll(tail_cur[rows], 1, 0)))
        xc = cb_ref[j:j + 1, cs] + cw_ref[CONV_W - 1:CONV_W, cs] * x
        for shift in range(1, CONV_W):
            tap = CONV_W - 1 - shift
            shifted = jnp.concatenate(wrap[CONV_W - 1 - shift:] + [x[:TM - shift * SUBLANES]], axis=0)
            xc = xc + cw_ref[tap:tap + 1, cs] * shifted
        xcb = xc.astype(_BF16)
        r = jax.nn.sigmoid(_dot(xcb, wa_ref[blk]) + ba_ref[j, blk:blk + 1, :])
        ig = jax.nn.sigmoid(_dot(xcb, wx_ref[blk]) + bx_ref[j, blk:blk + 1, :])
        sg = gate * jax.nn.sigmoid(gate)

        nla = (C_RG * r) * sp_ref[0:1, cs]
        a = jnp.exp(-nla)
        z = jnp.tanh(nla) * (a * a + 1.0)
        mult = jnp.where(z > 0.0, z * lax.rsqrt(z), 0.0)
        b = mult * (ig * xc)

        acc_a = a[0:SUBLANES]
        acc_h = b[0:SUBLANES]
        prods, local = [acc_a], [acc_h]
        for g in range(1, SEG):
            rows = slice(g * SUBLANES, (g + 1) * SUBLANES)
            acc_h = a[rows] * acc_h + b[rows]
            acc_a = a[rows] * acc_a
            prods.append(acc_a)
            local.append(acc_h)
        carry_in = hc_ref[:, cs]
        seg_a, seg_h = _sublane_scan(acc_a, acc_h, row)
        seg_end = seg_h + seg_a * carry_in
        entering = jnp.where(first_segment, carry_in, pltpu.roll(seg_end, 1, 0))
        hc_ref[:, cs] = jnp.broadcast_to(seg_end[SUBLANES - 1:SUBLANES, :], (SUBLANES, LRU_BLOCK_W))
        ys = [(local[g] + prods[g] * entering) * sg[g * SUBLANES:(g + 1) * SUBLANES]
              for g in range(SEG)]
        y_ref[cur, :, cs] = jnp.concatenate(ys, axis=0).astype(_BF16)

    def body(recur, finish):
        if finish:
            ynat_ref[...] = _dot(unperm_ref[...], y_ref[prv]).astype(_BF16)
            out_project(0)
            out_project(1)
        if recur:
            row = lax.broadcasted_iota(jnp.int32, (SUBLANES, LRU_BLOCK_W), 0)
            first_segment = row == 0
            u_nat = _rms(h_ref[...], gpre_ref[layer:layer + 1, :]).astype(_BF16)
            u_ref[...] = _dot(perm_ref[...], u_nat).astype(_BF16)
            project_x(0)
        for blk in range(N_LRU_BLOCKS):
            if recur and blk + 1 < N_LRU_BLOCKS:
                project_x(blk + 1)
            if finish and blk + 2 < N_LRU_BLOCKS:
                out_project(blk + 2)
            if recur:
                recur_block(blk, row, first_segment)
        if finish:
            out_ref[...] = hprev_ref[...] + _rms(yo_ref[...], gpost_ref[layer:layer + 1, :])

    _staggered(step, body)


def _lru_layer(h, layer, j, norm_pre, norm_post, small, weights, next_weights, j_next):
    lagged = lambda i: (jnp.maximum(i - 1, 0), 0)
    conv_w, conv_b, b_a, b_x, lam = small
    w_in, w_a, w_x, w_out = weights
    gate_shape = (N_LRU_BLOCKS, LRU_BLOCK_W, LRU_BLOCK_W)
    w_a, w_x = w_a.reshape(gate_shape), w_x.reshape(gate_shape)
    cast_in, cast_out, cast_shapes = _cast_specs(next_weights, j_next)
    return pl.pallas_call(
        functools.partial(_lru_kernel, layer, j, len(next_weights)),
        out_shape=[jax.ShapeDtypeStruct((SEQ, D_MODEL), _F32)] + cast_shapes,
        grid=(N_TILES + 1,),
        in_specs=[
            pl.BlockSpec((TM, D_MODEL), lambda i: (jnp.minimum(i, N_TILES - 1), 0)),
            pl.BlockSpec((TM, D_MODEL), lagged),
            _resident(norm_pre),
            _resident(norm_post),
            _resident(w_in), _layer_spec(conv_w, j), _resident(conv_b),
            _resident(w_a), _resident(b_a), _resident(w_x), _resident(b_x),
            _resident(lam), _resident(w_out),
        ] + cast_in,
        out_specs=[pl.BlockSpec((TM, D_MODEL), lagged)] + cast_out,
        scratch_shapes=[
            pltpu.VMEM((TM, D_MODEL), _BF16),
            pltpu.VMEM((TM, D_MODEL), _F32),
            pltpu.VMEM((CONV_TAIL, D_MODEL), _F32),
            pltpu.VMEM((2, TM, D_MODEL), _BF16),
            pltpu.VMEM((TM, D_MODEL), _BF16),
            pltpu.VMEM((TM, D_MODEL), _F32),
            pltpu.VMEM((SUBLANES, D_MODEL), _F32),
            pltpu.VMEM((SUBLANES, D_MODEL), _F32),
            pltpu.VMEM((TM, TM), _BF16),
            pltpu.VMEM((TM, TM), _BF16),
        ],
        compiler_params=pltpu.CompilerParams(
            dimension_semantics=("arbitrary",), vmem_limit_bytes=VMEM_LIMIT),
        name="lru_layer",
    )(h, h, norm_pre, norm_post, w_in, conv_w, conv_b, w_a, b_a, w_x, b_x, lam, w_out, *next_weights)


def kernel(x, norm_pre, norm_post, attn_w_in, attn_w_out, attn_sinks, lru_w_in, lru_conv_w,
           lru_conv_b, lru_w_a, lru_b_a, lru_w_x, lru_b_x, lru_lambda, lru_w_out):
    depth = norm_pre.shape[0]
    lru_small = (lru_conv_w, lru_conv_b, lru_b_a, lru_b_x, lru_lambda)
    rows = lambda w: w.reshape(w.shape[0], D_MODEL, -1)
    attn_f32 = (attn_w_in, attn_w_out)
    lru_f32 = (lru_w_in, rows(lru_w_a), rows(lru_w_x), lru_w_out)
    weights = tuple(w[0].astype(_BF16) for w in attn_f32)
    h = x.reshape(SEQ, D_MODEL)
    for layer in range(depth):
        j = layer // 2
        last = layer == depth - 1
        if layer % 2 == 0:
            h, *weights = _attn_layer(h, layer, j, norm_pre, norm_post, attn_sinks, *weights,
                                      () if last else lru_f32, j)
        else:
            h, *weights = _lru_layer(h, layer, j, norm_pre, norm_post, lru_small, weights,
                                     () if last else attn_f32, j + 1)
    return h.reshape(x.shape)
```

```python
import functools

import jax
import jax.numpy as jnp
from jax import lax
from jax.experimental import pallas as pl
from jax.experimental.pallas import tpu as pltpu

D_MODEL = 2048
SEQ = 8192
HEAD_DIM = 64
N_HEADS = 32
N_KV_HEADS = 4
GROUP = N_HEADS // N_KV_HEADS
BLOCK = 128
KV_W = N_KV_HEADS * HEAD_DIM
GATE_COL = D_MODEL + 2 * KV_W
LRU_BLOCK_W = 256
N_LRU_BLOCKS = D_MODEL // LRU_BLOCK_W
CONV_W = 4
C_RG = 8.0
NORM_EPS = 1e-6
MASK_VALUE = -1e30

LANES = 128
SUBLANES = 8
N_PAIRS = N_HEADS // 2
PAIRS_PER_KV = GROUP // 2

TM = 256
N_TILES = SEQ // TM
VMEM_LIMIT = 60 * 1024 * 1024

_F32 = jnp.float32
_BF16 = jnp.bfloat16


def _rms(x, g):
    ms = jnp.mean(x * x, axis=-1, keepdims=True)
    return x * lax.rsqrt(ms + NORM_EPS) * g


def _dot(a, b):
    return jnp.dot(a, b, preferred_element_type=_F32)


def _dot_nt(a, b):
    return lax.dot_general(a, b, (((1,), (1,)), ((), ())), preferred_element_type=_F32)


CAST_ROWS = D_MODEL // N_TILES


def _cast_rows(srcs, dsts):
    for src, dst in zip(srcs, dsts):
        dst[...] = src[...].astype(_BF16)


def _cast_specs(stacked, j):
    rows = lambda i: jnp.minimum(i, N_TILES - 1)
    in_specs = [pl.BlockSpec((None, CAST_ROWS, w.shape[-1]), lambda i: (j, rows(i), 0)) for w in stacked]
    out_specs = [pl.BlockSpec((CAST_ROWS, w.shape[-1]), lambda i: (rows(i), 0)) for w in stacked]
    out_shapes = [jax.ShapeDtypeStruct((D_MODEL, w.shape[-1]), _BF16) for w in stacked]
    return in_specs, out_specs, out_shapes


def _resident(arr):
    zeros = (0,) * arr.ndim
    return pl.BlockSpec(arr.shape, lambda i: zeros, pipeline_mode=pl.Buffered(1))


def _staggered(step, body):
    pl.when(step == 0)(lambda: body(True, False))
    pl.when(jnp.logical_and(step > 0, step < N_TILES))(lambda: body(True, True))
    pl.when(step == N_TILES)(lambda: body(False, True))


def _attn_kernel(j, n_cast, sink_ref, h_ref, hprev_ref, gpre_ref, gpost_ref, win_ref, wout_ref,
                 *refs):
    cast_src, (out_ref, *cast_dst), scratch = refs[:n_cast], refs[n_cast:2 * n_cast + 1], refs[2 * n_cast + 1:]
    u_ref, q3_ref, sg3_ref, y3_ref, klo_ref, khi_ref, vt_ref = scratch
    step = pl.program_id(0)
    cur = step % 2
    prv = 1 - cur
    nb = TM // BLOCK
    stack = PAIRS_PER_KV * BLOCK
    chunks = BLOCK // SUBLANES
    per_kv = GROUP * HEAD_DIM

    def project_kv(first):
        kv = _dot(u_ref[...], win_ref[:, D_MODEL:GATE_COL])
        lane_lo = lax.broadcasted_iota(jnp.int32, (TM, LANES), 1) < HEAD_DIM
        new = slice(BLOCK, BLOCK + TM)
        for c in range(N_KV_HEADS // 2):
            orig = kv[:, c * LANES:(c + 1) * LANES]
            sw = pltpu.roll(orig, HEAD_DIM, 1)
            klo_ref[cur, 2 * c, new, :] = jnp.where(lane_lo, orig, 0.0).astype(_BF16)
            khi_ref[cur, 2 * c, new, :] = jnp.where(lane_lo, 0.0, sw).astype(_BF16)
            klo_ref[cur, 2 * c + 1, new, :] = jnp.where(lane_lo, sw, 0.0).astype(_BF16)
            khi_ref[cur, 2 * c + 1, new, :] = jnp.where(lane_lo, 0.0, orig).astype(_BF16)
        vt_ref[cur, :, new] = kv[:, KV_W:2 * KV_W].T.astype(_BF16)
        for ref in (klo_ref, khi_ref):
            prev = jnp.zeros((N_KV_HEADS, BLOCK, LANES), _BF16) if first else ref[prv, :, TM:TM + BLOCK, :]
            ref[cur, :, 0:BLOCK, :] = prev
        vt_ref[cur, :, 0:BLOCK] = (jnp.zeros((KV_W, BLOCK), _BF16) if first
                                   else vt_ref[prv, :, TM:TM + BLOCK])

    def reduce_rows(x3, op):
        r = x3[0]
        for c in range(1, chunks):
            r = op(r, x3[c])
        for d in (1, 2, 4):
            r = op(r, pltpu.roll(r, d, 0))
        return r

    def softmax_band(s_full, sink_row, b, tri, prev_exists):
        s_prev = s_full[0:BLOCK]
        s_cur = s_full[BLOCK:2 * BLOCK]
        if b == 0:
            s_prev = jnp.where(prev_exists, s_prev, MASK_VALUE)
        s3 = jnp.where(tri, s_cur, s_prev).reshape(chunks, SUBLANES, stack)
        m = jnp.maximum(reduce_rows(s3, jnp.maximum), sink_row)
        e3 = jnp.exp(s3 - m[None])
        den = reduce_rows(e3, jnp.add) + jnp.exp(sink_row - m)
        pn = (e3 * (1.0 / den)[None]).reshape(BLOCK, stack)
        band = jnp.concatenate([jnp.where(tri, 0.0, pn), jnp.where(tri, pn, 0.0)], axis=0)
        return band.astype(_BF16)

    def sink_row(kvh, odd):
        cols = [jnp.full((SUBLANES, BLOCK), sink_ref[j, 2 * (PAIRS_PER_KV * kvh + i) + odd], _F32)
                for i in range(PAIRS_PER_KV)]
        return jnp.concatenate(cols, axis=1)

    def body(project, attend):
        _cast_rows(cast_src, cast_dst)
        if project:
            u_ref[...] = _rms(h_ref[...], gpre_ref[...]).astype(_BF16)
        if attend:
            kj = lax.broadcasted_iota(jnp.int32, (BLOCK, stack), 0)
            qi = lax.broadcasted_iota(jnp.int32, (BLOCK, stack), 1) & (BLOCK - 1)
            tri = kj <= qi
            prev_exists = (kj + ((step - 1) * TM - BLOCK)) >= 0

        for kvh in range(N_KV_HEADS):
            pairs = slice(PAIRS_PER_KV * kvh, PAIRS_PER_KV * (kvh + 1))
            if attend:
                scores = []
                for b in range(nb):
                    rows = slice(b * BLOCK, (b + 1) * BLOCK)
                    band = slice(b * BLOCK, (b + 2) * BLOCK)
                    qs = q3_ref[prv, pairs, rows, :].reshape(stack, LANES)
                    scores.append((_dot_nt(klo_ref[prv, kvh, band, :], qs),
                                   _dot_nt(khi_ref[prv, kvh, band, :], qs)))

            if project:
                q = _dot(u_ref[...], win_ref[:, kvh * per_kv:(kvh + 1) * per_kv])
                gate = _dot(u_ref[...],
                            win_ref[:, GATE_COL + kvh * per_kv:GATE_COL + (kvh + 1) * per_kv])
                for i in range(PAIRS_PER_KV):
                    p = PAIRS_PER_KV * kvh + i
                    q3_ref[cur, p] = (q[:, i * LANES:(i + 1) * LANES] * (1.0 / 8.0)).astype(_BF16)
                    g = gate[:, i * LANES:(i + 1) * LANES]
                    sg3_ref[cur, p] = g * jax.nn.sigmoid(g)

            if attend:
                sink_e = sink_row(kvh, 0)
                sink_o = sink_row(kvh, 1)
                for b in range(nb):
                    rows = slice(b * BLOCK, (b + 1) * BLOCK)
                    band = slice(b * BLOCK, (b + 2) * BLOCK)
                    vt = vt_ref[prv, kvh * HEAD_DIM:(kvh + 1) * HEAD_DIM, band]
                    o_e = _dot(vt, softmax_band(scores[b][0], sink_e, b, tri, prev_exists))
                    o_o = _dot(vt, softmax_band(scores[b][1], sink_o, b, tri, prev_exists))
                    o2 = jnp.concatenate([o_e, o_o], axis=0).T
                    sg = sg3_ref[prv, pairs, rows, :].reshape(stack, LANES)
                    y3_ref[pairs, rows, :] = (o2 * sg).astype(_BF16).reshape(PAIRS_PER_KV, BLOCK, LANES)

        if project:
            project_kv(first=not attend)

        if attend:
            y = jnp.concatenate([y3_ref[p] for p in range(N_PAIRS)], axis=1)
            yo = _dot(y, wout_ref[...])
            out_ref[...] = hprev_ref[...] + _rms(yo, gpost_ref[...])

    _staggered(step, body)


def _layer_spec(stacked, j):
    zeros = (0,) * (stacked.ndim - 1)
    return pl.BlockSpec((None,) + stacked.shape[1:], lambda i: (j,) + zeros,
                        pipeline_mode=pl.Buffered(1))


def _attn_layer(h, layer, j, norm_pre, norm_post, sinks, w_in, w_out, next_weights, j_next):
    band_rows = BLOCK + TM
    lagged = lambda i: (jnp.maximum(i - 1, 0), 0)
    cast_in, cast_out, cast_shapes = _cast_specs(next_weights, j_next)
    return pl.pallas_call(
        functools.partial(_attn_kernel, j, len(next_weights)),
        out_shape=[jax.ShapeDtypeStruct((SEQ, D_MODEL), _F32)] + cast_shapes,
        grid=(N_TILES + 1,),
        in_specs=[
            pl.BlockSpec(memory_space=pltpu.SMEM),
            pl.BlockSpec((TM, D_MODEL), lambda i: (jnp.minimum(i, N_TILES - 1), 0)),
            pl.BlockSpec((TM, D_MODEL), lagged),
            _layer_spec(norm_pre, layer),
            _layer_spec(norm_post, layer),
            _resident(w_in),
            _resident(w_out),
        ] + cast_in,
        out_specs=[pl.BlockSpec((TM, D_MODEL), lagged)] + cast_out,
        scratch_shapes=[
            pltpu.VMEM((TM, D_MODEL), _BF16),
            pltpu.VMEM((2, N_PAIRS, TM, LANES), _BF16),
            pltpu.VMEM((2, N_PAIRS, TM, LANES), _F32),
            pltpu.VMEM((N_PAIRS, TM, LANES), _BF16),
            pltpu.VMEM((2, N_KV_HEADS, band_rows, LANES), _BF16),
            pltpu.VMEM((2, N_KV_HEADS, band_rows, LANES), _BF16),
            pltpu.VMEM((2, KV_W, band_rows), _BF16),
        ],
        compiler_params=pltpu.CompilerParams(
            dimension_semantics=("arbitrary",), vmem_limit_bytes=VMEM_LIMIT),
        name="attn_layer",
    )(sinks, h, h, norm_pre, norm_post, w_in, w_out, *next_weights)


SEG = TM // SUBLANES
CONV_TAIL = (CONV_W - 1) * SUBLANES


def _sublane_scan(a, b, row):
    for d in (1, 2, 4):
        keep = row >= d
        a_sh = jnp.where(keep, pltpu.roll(a, d, 0), 1.0)
        b_sh = jnp.where(keep, pltpu.roll(b, d, 0), 0.0)
        b = a * b_sh + b
        a = a * a_sh
    return a, b


def _lru_kernel(n_cast, h_ref, hprev_ref, gpre_ref, gpost_ref, win_ref, cw_ref, cb_ref, wa_ref,
                ba_ref, wx_ref, bx_ref, lam_ref, wout_ref, *refs):
    cast_src, (out_ref, *cast_dst), scratch = refs[:n_cast], refs[n_cast:2 * n_cast + 1], refs[2 * n_cast + 1:]
    u_ref, x_ref, tail_ref, y_ref, ynat_ref, yo_ref, sp_ref, hc_ref, perm_ref, unperm_ref = scratch
    step = pl.program_id(0)
    cur = step % 2
    prv = 1 - cur

    @pl.when(step == 0)
    def _():
        tail_ref[...] = jnp.zeros(tail_ref.shape, _F32)
        hc_ref[...] = jnp.zeros(hc_ref.shape, _F32)
        nl = -lam_ref[...]
        softplus = jnp.maximum(nl, 0.0) + jnp.log1p(jnp.exp(-jnp.abs(nl)))
        sp_ref[...] = jnp.broadcast_to(softplus, sp_ref.shape)
        ri = lax.broadcasted_iota(jnp.int32, (TM, TM), 0)
        ci = lax.broadcasted_iota(jnp.int32, (TM, TM), 1)
        perm_ref[...] = (ci == (ri % SUBLANES) * SEG + ri // SUBLANES).astype(_BF16)
        unperm_ref[...] = (ci == (ri % SEG) * SUBLANES + ri // SEG).astype(_BF16)

    def block_cols(blk):
        return slice(blk * LRU_BLOCK_W, (blk + 1) * LRU_BLOCK_W)

    def project_x(blk):
        cs = block_cols(blk)
        x_ref[:, cs] = _dot(u_ref[...], win_ref[:, cs])

    def out_project(blk):
        cs = block_cols(blk)
        yo_ref[:, cs] = _dot(ynat_ref[...], wout_ref[:, cs])

    def recur_block(blk, row, first_segment):
        cs = block_cols(blk)
        gate = _dot(u_ref[...],
                    win_ref[:, D_MODEL + blk * LRU_BLOCK_W:D_MODEL + (blk + 1) * LRU_BLOCK_W])

        x = x_ref[:, cs]
        tail_cur = x[TM - CONV_TAIL:]
        tail_prev = tail_ref[:, cs]
        tail_ref[:, cs] = tail_cur
        wrap = []
        for i in range(CONV_W - 1):
            rows = slice(i * SUBLANES, (i + 1) * SUBLANES)
            wrap.append(jnp.where(first_segment, pltpu.roll(tail_prev[rows], 1, 0),
                                  pltpu.roll(tail_cur[rows], 1, 0)))
        xc = cb_ref[0:1, cs] + cw_ref[CONV_W - 1:CONV_W, cs] * x
        for shift in range(1, CONV_W):
            tap = CONV_W - 1 - shift
            shifted = jnp.concatenate(wrap[CONV_W - 1 - shift:] + [x[:TM - shift * SUBLANES]], axis=0)
            xc = xc + cw_ref[tap:tap + 1, cs] * shifted
        xcb = xc.astype(_BF16)
        r = jax.nn.sigmoid(_dot(xcb, wa_ref[blk]) + ba_ref[0:1, cs])
        ig = jax.nn.sigmoid(_dot(xcb, wx_ref[blk]) + bx_ref[0:1, cs])
        sg = gate * jax.nn.sigmoid(gate)

        nla = (C_RG * r) * sp_ref[0:1, cs]
        a = jnp.exp(-nla)
        z = jnp.tanh(nla) * (a * a + 1.0)
        mult = jnp.where(z > 0.0, z * lax.rsqrt(z), 0.0)
        b = mult * (ig * xc)

        acc_a = a[0:SUBLANES]
        acc_h = b[0:SUBLANES]
        prods, local = [acc_a], [acc_h]
        for g in range(1, SEG):
            rows = slice(g * SUBLANES, (g + 1) * SUBLANES)
            acc_h = a[rows] * acc_h + b[rows]
            acc_a = a[rows] * acc_a
            prods.append(acc_a)
            local.append(acc_h)
        carry_in = hc_ref[:, cs]
        seg_a, seg_h = _sublane_scan(acc_a, acc_h, row)
        seg_end = seg_h + seg_a * carry_in
        entering = jnp.where(first_segment, carry_in, pltpu.roll(seg_end, 1, 0))
        hc_ref[:, cs] = jnp.broadcast_to(seg_end[SUBLANES - 1:SUBLANES, :], (SUBLANES, LRU_BLOCK_W))
        ys = [(local[g] + prods[g] * entering) * sg[g * SUBLANES:(g + 1) * SUBLANES]
              for g in range(SEG)]
        y_ref[cur, :, cs] = jnp.concatenate(ys, axis=0).astype(_BF16)

    def body(recur, finish):
        _cast_rows(cast_src, cast_dst)
        if finish:
            ynat_ref[...] = _dot(unperm_ref[...], y_ref[prv]).astype(_BF16)
            out_project(0)
            out_project(1)
        if recur:
            row = lax.broadcasted_iota(jnp.int32, (SUBLANES, LRU_BLOCK_W), 0)
            first_segment = row == 0
            u_nat = _rms(h_ref[...], gpre_ref[...]).astype(_BF16)
            u_ref[...] = _dot(perm_ref[...], u_nat).astype(_BF16)
            project_x(0)
        for blk in range(N_LRU_BLOCKS):
            if recur and blk + 1 < N_LRU_BLOCKS:
                project_x(blk + 1)
            if finish and blk + 2 < N_LRU_BLOCKS:
                out_project(blk + 2)
            if recur:
                recur_block(blk, row, first_segment)
        if finish:
            out_ref[...] = hprev_ref[...] + _rms(yo_ref[...], gpost_ref[...])

    _staggered(step, body)


def _lru_layer(h, layer, j, norm_pre, norm_post, small, weights, next_weights, j_next):
    lagged = lambda i: (jnp.maximum(i - 1, 0), 0)
    conv_w, conv_b, b_a, b_x, lam = small
    w_in, w_a, w_x, w_out = weights
    gate_shape = (N_LRU_BLOCKS, LRU_BLOCK_W, LRU_BLOCK_W)
    w_a, w_x = w_a.reshape(gate_shape), w_x.reshape(gate_shape)
    cast_in, cast_out, cast_shapes = _cast_specs(next_weights, j_next)
    return pl.pallas_call(
        functools.partial(_lru_kernel, len(next_weights)),
        out_shape=[jax.ShapeDtypeStruct((SEQ, D_MODEL), _F32)] + cast_shapes,
        grid=(N_TILES + 1,),
        in_specs=[
            pl.BlockSpec((TM, D_MODEL), lambda i: (jnp.minimum(i, N_TILES - 1), 0)),
            pl.BlockSpec((TM, D_MODEL), lagged),
            _layer_spec(norm_pre, layer),
            _layer_spec(norm_post, layer),
            _resident(w_in), _layer_spec(conv_w, j), _layer_spec(conv_b, j),
            _resident(w_a), _layer_spec(b_a, j), _resident(w_x), _layer_spec(b_x, j),
            _layer_spec(lam, j), _resident(w_out),
        ] + cast_in,
        out_specs=[pl.BlockSpec((TM, D_MODEL), lagged)] + cast_out,
        scratch_shapes=[
            pltpu.VMEM((TM, D_MODEL), _BF16),
            pltpu.VMEM((TM, D_MODEL), _F32),
            pltpu.VMEM((CONV_TAIL, D_MODEL), _F32),
            pltpu.VMEM((2, TM, D_MODEL), _BF16),
            pltpu.VMEM((TM, D_MODEL), _BF16),
            pltpu.VMEM((TM, D_MODEL), _F32),
            pltpu.VMEM((SUBLANES, D_MODEL), _F32),
            pltpu.VMEM((SUBLANES, D_MODEL), _F32),
            pltpu.VMEM((TM, TM), _BF16),
            pltpu.VMEM((TM, TM), _BF16),
        ],
        compiler_params=pltpu.CompilerParams(
            dimension_semantics=("arbitrary",), vmem_limit_bytes=VMEM_LIMIT),
        name="lru_layer",
    )(h, h, norm_pre, norm_post, w_in, conv_w, conv_b, w_a, b_a, w_x, b_x, lam, w_out, *next_weights)


def kernel(x, norm_pre, norm_post, attn_w_in, attn_w_out, attn_sinks, lru_w_in, lru_conv_w,
           lru_conv_b, lru_w_a, lru_b_a, lru_w_x, lru_b_x, lru_lambda, lru_w_out):
    depth = norm_pre.shape[0]
    row = lambda v: v.reshape(v.shape[0], 1, D_MODEL)
    norm_pre, norm_post = row(norm_pre), row(norm_post)
    lru_small = (lru_conv_w, row(lru_conv_b), row(lru_b_a), row(lru_b_x), row(lru_lambda))
    rows = lambda w: w.reshape(w.shape[0], D_MODEL, -1)
    attn_f32 = (attn_w_in, attn_w_out)
    lru_f32 = (lru_w_in, rows(lru_w_a), rows(lru_w_x), lru_w_out)
    weights = tuple(w[0].astype(_BF16) for w in attn_f32)
    h = x.reshape(SEQ, D_MODEL)
    for layer in range(depth):
        j = layer // 2
        last = layer == depth - 1
        if layer % 2 == 0:
            h, *weights = _attn_layer(h, layer, j, norm_pre, norm_post, attn_sinks, *weights,
                                      () if last else lru_f32, j)
        else:
            h, *weights = _lru_layer(h, layer, j, norm_pre, norm_post, lru_small, weights,
                                     () if last else attn_f32, j + 1)
    return h.reshape(x.shape)
```

```python
import functools

import jax
import jax.numpy as jnp
from jax import lax
from jax.experimental import pallas as pl
from jax.experimental.pallas import tpu as pltpu

D_MODEL = 2048
SEQ = 8192
HEAD_DIM = 64
N_HEADS = 32
N_KV_HEADS = 4
GROUP = N_HEADS // N_KV_HEADS
BLOCK = 128
KV_W = N_KV_HEADS * HEAD_DIM
GATE_COL = D_MODEL + 2 * KV_W
LRU_BLOCK_W = 256
N_LRU_BLOCKS = D_MODEL // LRU_BLOCK_W
CONV_W = 4
C_RG = 8.0
NORM_EPS = 1e-6
MASK_VALUE = -1e30

LANES = 128
SUBLANES = 8
N_PAIRS = N_HEADS // 2
PAIRS_PER_KV = GROUP // 2

TM = 256
N_TILES = SEQ // TM
VMEM_LIMIT = 60 * 1024 * 1024

_F32 = jnp.float32
_BF16 = jnp.bfloat16


def _rms(x, g):
    ms = jnp.mean(x * x, axis=-1, keepdims=True)
    return x * lax.rsqrt(ms + NORM_EPS) * g


def _dot(a, b):
    return jnp.dot(a, b, preferred_element_type=_F32)


def _dot_nt(a, b):
    return lax.dot_general(a, b, (((1,), (1,)), ((), ())), preferred_element_type=_F32)


CAST_ROWS = D_MODEL // N_TILES


def _cast_rows(srcs, dsts):
    for src, dst in zip(srcs, dsts):
        dst[...] = src[...].astype(_BF16)


def _cast_specs(stacked, j):
    rows = lambda i: jnp.minimum(i, N_TILES - 1)
    in_specs = [pl.BlockSpec((None, CAST_ROWS, w.shape[-1]), lambda i: (j, rows(i), 0)) for w in stacked]
    out_specs = [pl.BlockSpec((CAST_ROWS, w.shape[-1]), lambda i: (rows(i), 0)) for w in stacked]
    out_shapes = [jax.ShapeDtypeStruct((D_MODEL, w.shape[-1]), _BF16) for w in stacked]
    return in_specs, out_specs, out_shapes


def _resident(arr):
    zeros = (0,) * arr.ndim
    return pl.BlockSpec(arr.shape, lambda i: zeros, pipeline_mode=pl.Buffered(1))


def _staggered(step, body):
    pl.when(step == 0)(lambda: body(True, False))
    pl.when(jnp.logical_and(step > 0, step < N_TILES))(lambda: body(True, True))
    pl.when(step == N_TILES)(lambda: body(False, True))


def _attn_kernel(j, n_cast, sink_ref, h_ref, hprev_ref, gpre_ref, gpost_ref, win_ref, wout_ref,
                 *refs):
    cast_src, (out_ref, *cast_dst), scratch = refs[:n_cast], refs[n_cast:2 * n_cast + 1], refs[2 * n_cast + 1:]
    u_ref, q3_ref, sg3_ref, y3_ref, klo_ref, khi_ref, vt_ref = scratch
    _cast_rows(cast_src, cast_dst)
    step = pl.program_id(0)
    cur = step % 2
    prv = 1 - cur
    nb = TM // BLOCK
    stack = PAIRS_PER_KV * BLOCK
    chunks = BLOCK // SUBLANES
    per_kv = GROUP * HEAD_DIM

    def project_kv(first):
        kv = _dot(u_ref[...], win_ref[:, D_MODEL:GATE_COL])
        lane_lo = lax.broadcasted_iota(jnp.int32, (TM, LANES), 1) < HEAD_DIM
        new = slice(BLOCK, BLOCK + TM)
        for c in range(N_KV_HEADS // 2):
            orig = kv[:, c * LANES:(c + 1) * LANES]
            sw = pltpu.roll(orig, HEAD_DIM, 1)
            klo_ref[cur, 2 * c, new, :] = jnp.where(lane_lo, orig, 0.0).astype(_BF16)
            khi_ref[cur, 2 * c, new, :] = jnp.where(lane_lo, 0.0, sw).astype(_BF16)
            klo_ref[cur, 2 * c + 1, new, :] = jnp.where(lane_lo, sw, 0.0).astype(_BF16)
            khi_ref[cur, 2 * c + 1, new, :] = jnp.where(lane_lo, 0.0, orig).astype(_BF16)
        vt_ref[cur, :, new] = kv[:, KV_W:2 * KV_W].T.astype(_BF16)
        for ref in (klo_ref, khi_ref):
            prev = jnp.zeros((N_KV_HEADS, BLOCK, LANES), _BF16) if first else ref[prv, :, TM:TM + BLOCK, :]
            ref[cur, :, 0:BLOCK, :] = prev
        vt_ref[cur, :, 0:BLOCK] = (jnp.zeros((KV_W, BLOCK), _BF16) if first
                                   else vt_ref[prv, :, TM:TM + BLOCK])

    def reduce_rows(x3, op):
        r = x3[0]
        for c in range(1, chunks):
            r = op(r, x3[c])
        for d in (1, 2, 4):
            r = op(r, pltpu.roll(r, d, 0))
        return r

    def softmax_band(s_full, sink_row, b, tri, prev_exists):
        s_prev = s_full[0:BLOCK]
        s_cur = s_full[BLOCK:2 * BLOCK]
        if b == 0:
            s_prev = jnp.where(prev_exists, s_prev, MASK_VALUE)
        s3 = jnp.where(tri, s_cur, s_prev).reshape(chunks, SUBLANES, stack)
        m = jnp.maximum(reduce_rows(s3, jnp.maximum), sink_row)
        e3 = jnp.exp(s3 - m[None])
        den = reduce_rows(e3, jnp.add) + jnp.exp(sink_row - m)
        pn = (e3 * (1.0 / den)[None]).reshape(BLOCK, stack)
        band = jnp.concatenate([jnp.where(tri, 0.0, pn), jnp.where(tri, pn, 0.0)], axis=0)
        return band.astype(_BF16)

    def sink_row(kvh, odd):
        cols = [jnp.full((SUBLANES, BLOCK), sink_ref[j, 2 * (PAIRS_PER_KV * kvh + i) + odd], _F32)
                for i in range(PAIRS_PER_KV)]
        return jnp.concatenate(cols, axis=1)

    def body(project, attend):
        if project:
            u_ref[...] = _rms(h_ref[...], gpre_ref[...]).astype(_BF16)
        if attend:
            kj = lax.broadcasted_iota(jnp.int32, (BLOCK, stack), 0)
            qi = lax.broadcasted_iota(jnp.int32, (BLOCK, stack), 1) & (BLOCK - 1)
            tri = kj <= qi
            prev_exists = (kj + ((step - 1) * TM - BLOCK)) >= 0

        for kvh in range(N_KV_HEADS):
            pairs = slice(PAIRS_PER_KV * kvh, PAIRS_PER_KV * (kvh + 1))
            if attend:
                scores = []
                for b in range(nb):
                    rows = slice(b * BLOCK, (b + 1) * BLOCK)
                    band = slice(b * BLOCK, (b + 2) * BLOCK)
                    qs = q3_ref[prv, pairs, rows, :].reshape(stack, LANES)
                    scores.append((_dot_nt(klo_ref[prv, kvh, band, :], qs),
                                   _dot_nt(khi_ref[prv, kvh, band, :], qs)))

            if project:
                q = _dot(u_ref[...], win_ref[:, kvh * per_kv:(kvh + 1) * per_kv])
                gate = _dot(u_ref[...],
                            win_ref[:, GATE_COL + kvh * per_kv:GATE_COL + (kvh + 1) * per_kv])
                for i in range(PAIRS_PER_KV):
                    p = PAIRS_PER_KV * kvh + i
                    q3_ref[cur, p] = (q[:, i * LANES:(i + 1) * LANES] * (1.0 / 8.0)).astype(_BF16)
                    g = gate[:, i * LANES:(i + 1) * LANES]
                    sg3_ref[cur, p] = g * jax.nn.sigmoid(g)

            if attend:
                sink_e = sink_row(kvh, 0)
                sink_o = sink_row(kvh, 1)
                for b in range(nb):
                    rows = slice(b * BLOCK, (b + 1) * BLOCK)
                    band = slice(b * BLOCK, (b + 2) * BLOCK)
                    vt = vt_ref[prv, kvh * HEAD_DIM:(kvh + 1) * HEAD_DIM, band]
                    o_e = _dot(vt, softmax_band(scores[b][0], sink_e, b, tri, prev_exists))
                    o_o = _dot(vt, softmax_band(scores[b][1], sink_o, b, tri, prev_exists))
                    o2 = jnp.concatenate([o_e, o_o], axis=0).T
                    sg = sg3_ref[prv, pairs, rows, :].reshape(stack, LANES)
                    y3_ref[pairs, rows, :] = (o2 * sg).astype(_BF16).reshape(PAIRS_PER_KV, BLOCK, LANES)

        if project:
            project_kv(first=not attend)

        if attend:
            y = jnp.concatenate([y3_ref[p] for p in range(N_PAIRS)], axis=1)
            yo = _dot(y, wout_ref[...])
            out_ref[...] = hprev_ref[...] + _rms(yo, gpost_ref[...])

    _staggered(step, body)


def _layer_spec(stacked, j):
    zeros = (0,) * (stacked.ndim - 1)
    return pl.BlockSpec((None,) + stacked.shape[1:], lambda i: (j,) + zeros,
                        pipeline_mode=pl.Buffered(1))


def _attn_layer(h, layer, j, norm_pre, norm_post, sinks, w_in, w_out, next_weights, j_next):
    band_rows = BLOCK + TM
    lagged = lambda i: (jnp.maximum(i - 1, 0), 0)
    cast_in, cast_out, cast_shapes = _cast_specs(next_weights, j_next)
    return pl.pallas_call(
        functools.partial(_attn_kernel, j, len(next_weights)),
        out_shape=[jax.ShapeDtypeStruct((SEQ, D_MODEL), _F32)] + cast_shapes,
        grid=(N_TILES + 1,),
        in_specs=[
            pl.BlockSpec(memory_space=pltpu.SMEM),
            pl.BlockSpec((TM, D_MODEL), lambda i: (jnp.minimum(i, N_TILES - 1), 0)),
            pl.BlockSpec((TM, D_MODEL), lagged),
            _layer_spec(norm_pre, layer),
            _layer_spec(norm_post, layer),
            _resident(w_in),
            _resident(w_out),
        ] + cast_in,
        out_specs=[pl.BlockSpec((TM, D_MODEL), lagged)] + cast_out,
        scratch_shapes=[
            pltpu.VMEM((TM, D_MODEL), _BF16),
            pltpu.VMEM((2, N_PAIRS, TM, LANES), _BF16),
            pltpu.VMEM((2, N_PAIRS, TM, LANES), _F32),
            pltpu.VMEM((N_PAIRS, TM, LANES), _BF16),
            pltpu.VMEM((2, N_KV_HEADS, band_rows, LANES), _BF16),
            pltpu.VMEM((2, N_KV_HEADS, band_rows, LANES), _BF16),
            pltpu.VMEM((2, KV_W, band_rows), _BF16),
        ],
        compiler_params=pltpu.CompilerParams(
            dimension_semantics=("arbitrary",), vmem_limit_bytes=VMEM_LIMIT),
        name="attn_layer",
    )(sinks, h, h, norm_pre, norm_post, w_in, w_out, *next_weights)


SEG = TM // SUBLANES
H_SLOTS = 3
CONV_TAIL = (CONV_W - 1) * SUBLANES


def _sublane_scan(a, b, row):
    for d in (1, 2, 4):
        keep = row >= d
        a_sh = jnp.where(keep, pltpu.roll(a, d, 0), 1.0)
        b_sh = jnp.where(keep, pltpu.roll(b, d, 0), 0.0)
        b = a * b_sh + b
        a = a * a_sh
    return a, b


def _lru_kernel(n_cast, h_hbm, gpre_ref, gpost_ref, win_ref, cw_ref, cb_ref, wa_ref,
                ba_ref, wx_ref, bx_ref, lam_ref, wout_ref, *refs):
    cast_src, (out_hbm, *cast_dst), scratch = refs[:n_cast], refs[n_cast:2 * n_cast + 1], refs[2 * n_cast + 1:]
    (u_ref, x_ref, tail_ref, y_ref, yo_ref, sp_ref, hc_ref,
     h_buf, o_buf, in_sem, out_sem) = scratch
    _cast_rows(cast_src, cast_dst)
    step = pl.program_id(0)
    cur = step % 2
    prv = 1 - cur

    def in_copies(tile, slot):
        return [pltpu.make_async_copy(h_hbm.at[pl.ds(tile * TM + s * SEG, SEG), :],
                                      h_buf.at[slot, :, s, :], in_sem.at[slot])
                for s in range(SUBLANES)]

    def out_copies(tile, slot):
        return [pltpu.make_async_copy(o_buf.at[slot, :, s, :],
                                      out_hbm.at[pl.ds(tile * TM + s * SEG, SEG), :], out_sem.at[slot])
                for s in range(SUBLANES)]

    @pl.when(step == 0)
    def _():
        for c in in_copies(0, 0):
            c.start()
        tail_ref[...] = jnp.zeros(tail_ref.shape, _F32)
        hc_ref[...] = jnp.zeros(hc_ref.shape, _F32)
        nl = -lam_ref[...]
        softplus = jnp.maximum(nl, 0.0) + jnp.log1p(jnp.exp(-jnp.abs(nl)))
        sp_ref[...] = jnp.broadcast_to(softplus, sp_ref.shape)

    def block_cols(blk):
        return slice(blk * LRU_BLOCK_W, (blk + 1) * LRU_BLOCK_W)

    def project_x(blk):
        cs = block_cols(blk)
        x_ref[:, cs] = _dot(u_ref[...], win_ref[:, cs])

    def out_project(blk):
        cs = block_cols(blk)
        yo_ref[:, cs] = _dot(y_ref[prv], wout_ref[:, cs])

    def recur_block(blk, row, first_segment):
        cs = block_cols(blk)
        gate = _dot(u_ref[...],
                    win_ref[:, D_MODEL + blk * LRU_BLOCK_W:D_MODEL + (blk + 1) * LRU_BLOCK_W])

        x = x_ref[:, cs]
        tail_cur = x[TM - CONV_TAIL:]
        tail_prev = tail_ref[:, cs]
        tail_ref[:, cs] = tail_cur
        wrap = []
        for i in range(CONV_W - 1):
            rows = slice(i * SUBLANES, (i + 1) * SUBLANES)
            wrap.append(jnp.where(first_segment, pltpu.roll(tail_prev[rows], 1, 0),
                                  pltpu.roll(tail_cur[rows], 1, 0)))
        xc = cb_ref[0:1, cs] + cw_ref[CONV_W - 1:CONV_W, cs] * x
        for shift in range(1, CONV_W):
            tap = CONV_W - 1 - shift
            shifted = jnp.concatenate(wrap[CONV_W - 1 - shift:] + [x[:TM - shift * SUBLANES]], axis=0)
            xc = xc + cw_ref[tap:tap + 1, cs] * shifted
        xcb = xc.astype(_BF16)
        r = jax.nn.sigmoid(_dot(xcb, wa_ref[blk]) + ba_ref[0:1, cs])
        ig = jax.nn.sigmoid(_dot(xcb, wx_ref[blk]) + bx_ref[0:1, cs])
        sg = gate * jax.nn.sigmoid(gate)

        nla = (C_RG * r) * sp_ref[0:1, cs]
        a = jnp.exp(-nla)
        z = jnp.tanh(nla) * (a * a + 1.0)
        mult = jnp.where(z > 0.0, z * lax.rsqrt(z), 0.0)
        b = mult * (ig * xc)

        acc_a = a[0:SUBLANES]
        acc_h = b[0:SUBLANES]
        prods, local = [acc_a], [acc_h]
        for g in range(1, SEG):
            rows = slice(g * SUBLANES, (g + 1) * SUBLANES)
            acc_h = a[rows] * acc_h + b[rows]
            acc_a = a[rows] * acc_a
            prods.append(acc_a)
            local.append(acc_h)
        carry_in = hc_ref[:, cs]
        seg_a, seg_h = _sublane_scan(acc_a, acc_h, row)
        seg_end = seg_h + seg_a * carry_in
        entering = jnp.where(first_segment, carry_in, pltpu.roll(seg_end, 1, 0))
        hc_ref[:, cs] = jnp.broadcast_to(seg_end[SUBLANES - 1:SUBLANES, :], (SUBLANES, LRU_BLOCK_W))
        ys = [(local[g] + prods[g] * entering) * sg[g * SUBLANES:(g + 1) * SUBLANES]
              for g in range(SEG)]
        y_ref[cur, :, cs] = jnp.concatenate(ys, axis=0).astype(_BF16)

    def body(recur, finish):
        if recur:
            slot = step % H_SLOTS
            for c in in_copies(step, slot):
                c.wait()

            @pl.when(step + 1 < N_TILES)
            def _():
                for c in in_copies(step + 1, (step + 1) % H_SLOTS):
                    c.start()
        if finish:
            out_project(0)
            out_project(1)
        if recur:
            row = lax.broadcasted_iota(jnp.int32, (SUBLANES, LRU_BLOCK_W), 0)
            first_segment = row == 0
            u_ref[...] = _rms(h_buf[slot].reshape(TM, D_MODEL), gpre_ref[...]).astype(_BF16)
            project_x(0)
        for blk in range(N_LRU_BLOCKS):
            if recur and blk + 1 < N_LRU_BLOCKS:
                project_x(blk + 1)
            if finish and blk + 2 < N_LRU_BLOCKS:
                out_project(blk + 2)
            if recur:
                recur_block(blk, row, first_segment)
        if finish:
            oslot = prv

            @pl.when(step >= 3)
            def _():
                for c in out_copies(step - 3, oslot):
                    c.wait()
            h_prev = h_buf[(step - 1) % H_SLOTS].reshape(TM, D_MODEL)
            out = h_prev + _rms(yo_ref[...], gpost_ref[...])
            o_buf[oslot] = out.reshape(SEG, SUBLANES, D_MODEL)
            for c in out_copies(step - 1, oslot):
                c.start()
            if not recur:
                for c in out_copies(step - 2, cur) + out_copies(step - 1, oslot):
                    c.wait()

    _staggered(step, body)


def _lru_layer(h, layer, j, norm_pre, norm_post, small, weights, next_weights, j_next):
    conv_w, conv_b, b_a, b_x, lam = small
    w_in, w_a, w_x, w_out = weights
    gate_shape = (N_LRU_BLOCKS, LRU_BLOCK_W, LRU_BLOCK_W)
    w_a, w_x = w_a.reshape(gate_shape), w_x.reshape(gate_shape)
    cast_in, cast_out, cast_shapes = _cast_specs(next_weights, j_next)
    return pl.pallas_call(
        functools.partial(_lru_kernel, len(next_weights)),
        out_shape=[jax.ShapeDtypeStruct((SEQ, D_MODEL), _F32)] + cast_shapes,
        grid=(N_TILES + 1,),
        in_specs=[
            pl.BlockSpec(memory_space=pl.ANY),
            _layer_spec(norm_pre, layer),
            _layer_spec(norm_post, layer),
            _resident(w_in), _layer_spec(conv_w, j), _layer_spec(conv_b, j),
            _resident(w_a), _layer_spec(b_a, j), _resident(w_x), _layer_spec(b_x, j),
            _layer_spec(lam, j), _resident(w_out),
        ] + cast_in,
        out_specs=[pl.BlockSpec(memory_space=pl.ANY)] + cast_out,
        scratch_shapes=[
            pltpu.VMEM((TM, D_MODEL), _BF16),
            pltpu.VMEM((TM, D_MODEL), _F32),
            pltpu.VMEM((CONV_TAIL, D_MODEL), _F32),
            pltpu.VMEM((2, TM, D_MODEL), _BF16),
            pltpu.VMEM((TM, D_MODEL), _F32),
            pltpu.VMEM((SUBLANES, D_MODEL), _F32),
            pltpu.VMEM((SUBLANES, D_MODEL), _F32),
            pltpu.VMEM((H_SLOTS, SEG, SUBLANES, D_MODEL), _F32),
            pltpu.VMEM((2, SEG, SUBLANES, D_MODEL), _F32),
            pltpu.SemaphoreType.DMA((H_SLOTS,)),
            pltpu.SemaphoreType.DMA((2,)),
        ],
        compiler_params=pltpu.CompilerParams(
            dimension_semantics=("arbitrary",), vmem_limit_bytes=VMEM_LIMIT),
        name="lru_layer",
    )(h, norm_pre, norm_post, w_in, conv_w, conv_b, w_a, b_a, w_x, b_x, lam, w_out, *next_weights)


def kernel(x, norm_pre, norm_post, attn_w_in, attn_w_out, attn_sinks, lru_w_in, lru_conv_w,
           lru_conv_b, lru_w_a, lru_b_a, lru_w_x, lru_b_x, lru_lambda, lru_w_out):
    depth = norm_pre.shape[0]
    row = lambda v: v.reshape(v.shape[0], 1, D_MODEL)
    norm_pre, norm_post = row(norm_pre), row(norm_post)
    lru_small = (lru_conv_w, row(lru_conv_b), row(lru_b_a), row(lru_b_x), row(lru_lambda))
    rows = lambda w: w.reshape(w.shape[0], D_MODEL, -1)
    attn_f32 = (attn_w_in, attn_w_out)
    lru_f32 = (lru_w_in, rows(lru_w_a), rows(lru_w_x), lru_w_out)
    weights = tuple(w[0].astype(_BF16) for w in attn_f32)
    h = x.reshape(SEQ, D_MODEL)
    for layer in range(depth):
        j = layer // 2
        last = layer == depth - 1
        if layer % 2 == 0:
            h, *weights = _attn_layer(h, layer, j, norm_pre, norm_post, attn_sinks, *weights,
                                      () if last else lru_f32, j)
        else:
            h, *weights = _lru_layer(h, layer, j, norm_pre, norm_post, lru_small, weights,
                                     () if last else attn_f32, j + 1)
    return h.reshape(x.shape)
```

```python
import functools

import jax
import jax.numpy as jnp
from jax import lax
from jax.experimental import pallas as pl
from jax.experimental.pallas import tpu as pltpu

D_MODEL = 2048
SEQ = 8192
HEAD_DIM = 64
N_HEADS = 32
N_KV_HEADS = 4
GROUP = N_HEADS // N_KV_HEADS
BLOCK = 128
KV_W = N_KV_HEADS * HEAD_DIM
GATE_COL = D_MODEL + 2 * KV_W
LRU_BLOCK_W = 256
N_LRU_BLOCKS = D_MODEL // LRU_BLOCK_W
CONV_W = 4
C_RG = 8.0
NORM_EPS = 1e-6
MASK_VALUE = -1e30

LANES = 128
SUBLANES = 8
N_PAIRS = N_HEADS // 2
PAIRS_PER_KV = GROUP // 2

TM = 256
N_TILES = SEQ // TM
VMEM_LIMIT = 60 * 1024 * 1024

_F32 = jnp.float32
_BF16 = jnp.bfloat16


def _rms(x, g):
    ms = jnp.mean(x * x, axis=-1, keepdims=True)
    return x * lax.rsqrt(ms + NORM_EPS) * g


def _dot(a, b):
    return jnp.dot(a, b, preferred_element_type=_F32)


def _dot_nt(a, b):
    return lax.dot_general(a, b, (((1,), (1,)), ((), ())), preferred_element_type=_F32)


CAST_ROWS = D_MODEL // N_TILES


def _cast_rows(srcs, dsts):
    for src, dst in zip(srcs, dsts):
        dst[...] = src[...].astype(_BF16)


def _cast_specs(stacked, j):
    rows = lambda i: jnp.minimum(i, N_TILES - 1)
    in_specs = [pl.BlockSpec((None, CAST_ROWS, w.shape[-1]), lambda i: (j, rows(i), 0)) for w in stacked]
    out_specs = [pl.BlockSpec((CAST_ROWS, w.shape[-1]), lambda i: (rows(i), 0)) for w in stacked]
    out_shapes = [jax.ShapeDtypeStruct((D_MODEL, w.shape[-1]), _BF16) for w in stacked]
    return in_specs, out_specs, out_shapes


def _resident(arr):
    zeros = (0,) * arr.ndim
    return pl.BlockSpec(arr.shape, lambda i: zeros, pipeline_mode=pl.Buffered(1))


def _staggered(step, body):
    pl.when(step == 0)(lambda: body(True, False))
    pl.when(jnp.logical_and(step > 0, step < N_TILES))(lambda: body(True, True))
    pl.when(step == N_TILES)(lambda: body(False, True))


def _attn_kernel(j, n_cast, sink_ref, h_ref, hprev_ref, gpre_ref, gpost_ref, win_ref, wout_ref,
                 *refs):
    cast_src, (out_ref, *cast_dst), scratch = refs[:n_cast], refs[n_cast:2 * n_cast + 1], refs[2 * n_cast + 1:]
    u_ref, q3_ref, sg3_ref, y3_ref, klo_ref, khi_ref, vt_ref = scratch
    _cast_rows(cast_src, cast_dst)
    step = pl.program_id(0)
    cur = step % 2
    prv = 1 - cur
    nb = TM // BLOCK
    stack = PAIRS_PER_KV * BLOCK
    chunks = BLOCK // SUBLANES
    per_kv = GROUP * HEAD_DIM

    def project_kv(first):
        kv = _dot(u_ref[...], win_ref[:, D_MODEL:GATE_COL])
        lane_lo = lax.broadcasted_iota(jnp.int32, (TM, LANES), 1) < HEAD_DIM
        new = slice(BLOCK, BLOCK + TM)
        for c in range(N_KV_HEADS // 2):
            orig = kv[:, c * LANES:(c + 1) * LANES]
            sw = pltpu.roll(orig, HEAD_DIM, 1)
            klo_ref[cur, 2 * c, new, :] = jnp.where(lane_lo, orig, 0.0).astype(_BF16)
            khi_ref[cur, 2 * c, new, :] = jnp.where(lane_lo, 0.0, sw).astype(_BF16)
            klo_ref[cur, 2 * c + 1, new, :] = jnp.where(lane_lo, sw, 0.0).astype(_BF16)
            khi_ref[cur, 2 * c + 1, new, :] = jnp.where(lane_lo, 0.0, orig).astype(_BF16)
        vt_ref[cur, :, new] = kv[:, KV_W:2 * KV_W].T.astype(_BF16)
        for ref in (klo_ref, khi_ref):
            prev = jnp.zeros((N_KV_HEADS, BLOCK, LANES), _BF16) if first else ref[prv, :, TM:TM + BLOCK, :]
            ref[cur, :, 0:BLOCK, :] = prev
        vt_ref[cur, :, 0:BLOCK] = (jnp.zeros((KV_W, BLOCK), _BF16) if first
                                   else vt_ref[prv, :, TM:TM + BLOCK])

    def reduce_rows(x3, op):
        r = x3[0]
        for c in range(1, chunks):
            r = op(r, x3[c])
        for d in (1, 2, 4):
            r = op(r, pltpu.roll(r, d, 0))
        return r

    def softmax_band(s_full, sink_row, b, tri, prev_exists):
        s_prev = s_full[0:BLOCK]
        s_cur = s_full[BLOCK:2 * BLOCK]
        if b == 0:
            s_prev = jnp.where(prev_exists, s_prev, MASK_VALUE)
        s3 = jnp.where(tri, s_cur, s_prev).reshape(chunks, SUBLANES, stack)
        m = jnp.maximum(reduce_rows(s3, jnp.maximum), sink_row)
        e3 = jnp.exp(s3 - m[None])
        den = reduce_rows(e3, jnp.add) + jnp.exp(sink_row - m)
        pn = (e3 * (1.0 / den)[None]).reshape(BLOCK, stack)
        band = jnp.concatenate([jnp.where(tri, 0.0, pn), jnp.where(tri, pn, 0.0)], axis=0)
        return band.astype(_BF16)

    def sink_row(kvh, odd):
        cols = [jnp.full((SUBLANES, BLOCK), sink_ref[j, 2 * (PAIRS_PER_KV * kvh + i) + odd], _F32)
                for i in range(PAIRS_PER_KV)]
        return jnp.concatenate(cols, axis=1)

    def body(project, attend):
        if project:
            u_ref[...] = _rms(h_ref[...], gpre_ref[...]).astype(_BF16)
        if attend:
            kj = lax.broadcasted_iota(jnp.int32, (BLOCK, stack), 0)
            qi = lax.broadcasted_iota(jnp.int32, (BLOCK, stack), 1) & (BLOCK - 1)
            tri = kj <= qi
            prev_exists = (kj + ((step - 1) * TM - BLOCK)) >= 0

        def weighted_values(kvh, probs):
            pairs = slice(PAIRS_PER_KV * kvh, PAIRS_PER_KV * (kvh + 1))
            for b in range(nb):
                rows = slice(b * BLOCK, (b + 1) * BLOCK)
                band = slice(b * BLOCK, (b + 2) * BLOCK)
                vt = vt_ref[prv, kvh * HEAD_DIM:(kvh + 1) * HEAD_DIM, band]
                o_e = _dot(vt, probs[b][0])
                o_o = _dot(vt, probs[b][1])
                o2 = jnp.concatenate([o_e, o_o], axis=0).T
                sg = sg3_ref[prv, pairs, rows, :].reshape(stack, LANES)
                y3_ref[pairs, rows, :] = (o2 * sg).astype(_BF16).reshape(PAIRS_PER_KV, BLOCK, LANES)

        pending = None
        for kvh in range(N_KV_HEADS):
            pairs = slice(PAIRS_PER_KV * kvh, PAIRS_PER_KV * (kvh + 1))
            if attend:
                scores = []
                for b in range(nb):
                    rows = slice(b * BLOCK, (b + 1) * BLOCK)
                    band = slice(b * BLOCK, (b + 2) * BLOCK)
                    qs = q3_ref[prv, pairs, rows, :].reshape(stack, LANES)
                    scores.append((_dot_nt(klo_ref[prv, kvh, band, :], qs),
                                   _dot_nt(khi_ref[prv, kvh, band, :], qs)))

            if project:
                q = _dot(u_ref[...], win_ref[:, kvh * per_kv:(kvh + 1) * per_kv])
            if attend and pending is not None:
                weighted_values(*pending)
            if project:
                gate = _dot(u_ref[...],
                            win_ref[:, GATE_COL + kvh * per_kv:GATE_COL + (kvh + 1) * per_kv])
                for i in range(PAIRS_PER_KV):
                    p = PAIRS_PER_KV * kvh + i
                    q3_ref[cur, p] = (q[:, i * LANES:(i + 1) * LANES] * (1.0 / 8.0)).astype(_BF16)
                    g = gate[:, i * LANES:(i + 1) * LANES]
                    sg3_ref[cur, p] = g * jax.nn.sigmoid(g)

            if attend:
                sink_e = sink_row(kvh, 0)
                sink_o = sink_row(kvh, 1)
                pending = (kvh, [(softmax_band(scores[b][0], sink_e, b, tri, prev_exists),
                                  softmax_band(scores[b][1], sink_o, b, tri, prev_exists))
                                 for b in range(nb)])

        if project:
            project_kv(first=not attend)

        if attend:
            weighted_values(*pending)
            y = jnp.concatenate([y3_ref[p] for p in range(N_PAIRS)], axis=1)
            yo = _dot(y, wout_ref[...])
            out_ref[...] = hprev_ref[...] + _rms(yo, gpost_ref[...])

    _staggered(step, body)


def _layer_spec(stacked, j):
    zeros = (0,) * (stacked.ndim - 1)
    return pl.BlockSpec((None,) + stacked.shape[1:], lambda i: (j,) + zeros,
                        pipeline_mode=pl.Buffered(1))


def _attn_layer(h, layer, j, norm_pre, norm_post, sinks, w_in, w_out, next_weights, j_next):
    band_rows = BLOCK + TM
    lagged = lambda i: (jnp.maximum(i - 1, 0), 0)
    cast_in, cast_out, cast_shapes = _cast_specs(next_weights, j_next)
    return pl.pallas_call(
        functools.partial(_attn_kernel, j, len(next_weights)),
        out_shape=[jax.ShapeDtypeStruct((SEQ, D_MODEL), _F32)] + cast_shapes,
        grid=(N_TILES + 1,),
        in_specs=[
            pl.BlockSpec(memory_space=pltpu.SMEM),
            pl.BlockSpec((TM, D_MODEL), lambda i: (jnp.minimum(i, N_TILES - 1), 0)),
            pl.BlockSpec((TM, D_MODEL), lagged),
            _layer_spec(norm_pre, layer),
            _layer_spec(norm_post, layer),
            _resident(w_in),
            _resident(w_out),
        ] + cast_in,
        out_specs=[pl.BlockSpec((TM, D_MODEL), lagged)] + cast_out,
        scratch_shapes=[
            pltpu.VMEM((TM, D_MODEL), _BF16),
            pltpu.VMEM((2, N_PAIRS, TM, LANES), _BF16),
            pltpu.VMEM((2, N_PAIRS, TM, LANES), _F32),
            pltpu.VMEM((N_PAIRS, TM, LANES), _BF16),
            pltpu.VMEM((2, N_KV_HEADS, band_rows, LANES), _BF16),
            pltpu.VMEM((2, N_KV_HEADS, band_rows, LANES), _BF16),
            pltpu.VMEM((2, KV_W, band_rows), _BF16),
        ],
        compiler_params=pltpu.CompilerParams(
            dimension_semantics=("arbitrary",), vmem_limit_bytes=VMEM_LIMIT),
        name="attn_layer",
    )(sinks, h, h, norm_pre, norm_post, w_in, w_out, *next_weights)


SEG = TM // SUBLANES
H_SLOTS = 3
CONV_TAIL = (CONV_W - 1) * SUBLANES


def _sublane_scan(a, b, row):
    for d in (1, 2, 4):
        keep = row >= d
        a_sh = jnp.where(keep, pltpu.roll(a, d, 0), 1.0)
        b_sh = jnp.where(keep, pltpu.roll(b, d, 0), 0.0)
        b = a * b_sh + b
        a = a * a_sh
    return a, b


def _lru_kernel(n_cast, h_hbm, gpre_ref, gpost_ref, win_ref, cw_ref, cb_ref, wa_ref,
                ba_ref, wx_ref, bx_ref, lam_ref, wout_ref, *refs):
    cast_src, (out_hbm, *cast_dst), scratch = refs[:n_cast], refs[n_cast:2 * n_cast + 1], refs[2 * n_cast + 1:]
    (u_ref, x_ref, tail_ref, y_ref, yo_ref, sp_ref, hc_ref,
     h_buf, o_buf, in_sem, out_sem) = scratch
    _cast_rows(cast_src, cast_dst)
    step = pl.program_id(0)
    cur = step % 2
    prv = 1 - cur

    def in_copies(tile, slot):
        return [pltpu.make_async_copy(h_hbm.at[pl.ds(tile * TM + s * SEG, SEG), :],
                                      h_buf.at[slot, :, s, :], in_sem.at[slot])
                for s in range(SUBLANES)]

    def out_copies(tile, slot):
        return [pltpu.make_async_copy(o_buf.at[slot, :, s, :],
                                      out_hbm.at[pl.ds(tile * TM + s * SEG, SEG), :], out_sem.at[slot])
                for s in range(SUBLANES)]

    @pl.when(step == 0)
    def _():
        for c in in_copies(0, 0):
            c.start()
        tail_ref[...] = jnp.zeros(tail_ref.shape, _F32)
        hc_ref[...] = jnp.zeros(hc_ref.shape, _F32)
        nl = -lam_ref[...]
        softplus = jnp.maximum(nl, 0.0) + jnp.log1p(jnp.exp(-jnp.abs(nl)))
        sp_ref[...] = jnp.broadcast_to(softplus, sp_ref.shape)

    def block_cols(blk):
        return slice(blk * LRU_BLOCK_W, (blk + 1) * LRU_BLOCK_W)

    def project_x(blk):
        cs = block_cols(blk)
        x_ref[:, cs] = _dot(u_ref[...], win_ref[:, cs])

    def out_project(blk):
        cs = block_cols(blk)
        yo_ref[:, cs] = _dot(y_ref[prv], wout_ref[:, cs])

    def recur_block(blk, row, first_segment):
        cs = block_cols(blk)
        gate = _dot(u_ref[...],
                    win_ref[:, D_MODEL + blk * LRU_BLOCK_W:D_MODEL + (blk + 1) * LRU_BLOCK_W])

        x = x_ref[:, cs]
        tail_cur = x[TM - CONV_TAIL:]
        tail_prev = tail_ref[:, cs]
        tail_ref[:, cs] = tail_cur
        wrap = []
        for i in range(CONV_W - 1):
            rows = slice(i * SUBLANES, (i + 1) * SUBLANES)
            wrap.append(jnp.where(first_segment, pltpu.roll(tail_prev[rows], 1, 0),
                                  pltpu.roll(tail_cur[rows], 1, 0)))
        xc = cb_ref[0:1, cs] + cw_ref[CONV_W - 1:CONV_W, cs] * x
        for shift in range(1, CONV_W):
            tap = CONV_W - 1 - shift
            shifted = jnp.concatenate(wrap[CONV_W - 1 - shift:] + [x[:TM - shift * SUBLANES]], axis=0)
            xc = xc + cw_ref[tap:tap + 1, cs] * shifted
        xcb = xc.astype(_BF16)
        r = jax.nn.sigmoid(_dot(xcb, wa_ref[blk]) + ba_ref[0:1, cs])
        ig = jax.nn.sigmoid(_dot(xcb, wx_ref[blk]) + bx_ref[0:1, cs])
        sg = gate * jax.nn.sigmoid(gate)

        nla = (C_RG * r) * sp_ref[0:1, cs]
        a = jnp.exp(-nla)
        z = jnp.tanh(nla) * (a * a + 1.0)
        mult = jnp.where(z > 0.0, z * lax.rsqrt(z), 0.0)
        b = mult * (ig * xc)

        acc_a = a[0:SUBLANES]
        acc_h = b[0:SUBLANES]
        prods, local = [acc_a], [acc_h]
        for g in range(1, SEG):
            rows = slice(g * SUBLANES, (g + 1) * SUBLANES)
            acc_h = a[rows] * acc_h + b[rows]
            acc_a = a[rows] * acc_a
            prods.append(acc_a)
            local.append(acc_h)
        carry_in = hc_ref[:, cs]
        seg_a, seg_h = _sublane_scan(acc_a, acc_h, row)
        seg_end = seg_h + seg_a * carry_in
        entering = jnp.where(first_segment, carry_in, pltpu.roll(seg_end, 1, 0))
        hc_ref[:, cs] = jnp.broadcast_to(seg_end[SUBLANES - 1:SUBLANES, :], (SUBLANES, LRU_BLOCK_W))
        ys = [(local[g] + prods[g] * entering) * sg[g * SUBLANES:(g + 1) * SUBLANES]
              for g in range(SEG)]
        y_ref[cur, :, cs] = jnp.concatenate(ys, axis=0).astype(_BF16)

    def body(recur, finish):
        if recur:
            slot = step % H_SLOTS
            for c in in_copies(step, slot):
                c.wait()

            @pl.when(step + 1 < N_TILES)
            def _():
                for c in in_copies(step + 1, (step + 1) % H_SLOTS):
                    c.start()
        if finish:
            out_project(0)
            out_project(1)
        if recur:
            row = lax.broadcasted_iota(jnp.int32, (SUBLANES, LRU_BLOCK_W), 0)
            first_segment = row == 0
            u_ref[...] = _rms(h_buf[slot].reshape(TM, D_MODEL), gpre_ref[...]).astype(_BF16)
            project_x(0)
        for blk in range(N_LRU_BLOCKS):
            if recur and blk + 1 < N_LRU_BLOCKS:
                project_x(blk + 1)
            if finish and blk + 2 < N_LRU_BLOCKS:
                out_project(blk + 2)
            if recur:
                recur_block(blk, row, first_segment)
        if finish:
            oslot = prv

            @pl.when(step >= 3)
            def _():
                for c in out_copies(step - 3, oslot):
                    c.wait()
            h_prev = h_buf[(step - 1) % H_SLOTS].reshape(TM, D_MODEL)
            out = h_prev + _rms(yo_ref[...], gpost_ref[...])
            o_buf[oslot] = out.reshape(SEG, SUBLANES, D_MODEL)
            for c in out_copies(step - 1, oslot):
                c.start()
            if not recur:
                for c in out_copies(step - 2, cur) + out_copies(step - 1, oslot):
                    c.wait()

    _staggered(step, body)


def _lru_layer(h, layer, j, norm_pre, norm_post, small, weights, next_weights, j_next):
    conv_w, conv_b, b_a, b_x, lam = small
    w_in, w_a, w_x, w_out = weights
    gate_shape = (N_LRU_BLOCKS, LRU_BLOCK_W, LRU_BLOCK_W)
    w_a, w_x = w_a.reshape(gate_shape), w_x.reshape(gate_shape)
    cast_in, cast_out, cast_shapes = _cast_specs(next_weights, j_next)
    return pl.pallas_call(
        functools.partial(_lru_kernel, len(next_weights)),
        out_shape=[jax.ShapeDtypeStruct((SEQ, D_MODEL), _F32)] + cast_shapes,
        grid=(N_TILES + 1,),
        in_specs=[
            pl.BlockSpec(memory_space=pl.ANY),
            _layer_spec(norm_pre, layer),
            _layer_spec(norm_post, layer),
            _resident(w_in), _layer_spec(conv_w, j), _layer_spec(conv_b, j),
            _resident(w_a), _layer_spec(b_a, j), _resident(w_x), _layer_spec(b_x, j),
            _layer_spec(lam, j), _resident(w_out),
        ] + cast_in,
        out_specs=[pl.BlockSpec(memory_space=pl.ANY)] + cast_out,
        scratch_shapes=[
            pltpu.VMEM((TM, D_MODEL), _BF16),
            pltpu.VMEM((TM, D_MODEL), _F32),
            pltpu.VMEM((CONV_TAIL, D_MODEL), _F32),
            pltpu.VMEM((2, TM, D_MODEL), _BF16),
            pltpu.VMEM((TM, D_MODEL), _F32),
            pltpu.VMEM((SUBLANES, D_MODEL), _F32),
            pltpu.VMEM((SUBLANES, D_MODEL), _F32),
            pltpu.VMEM((H_SLOTS, SEG, SUBLANES, D_MODEL), _F32),
            pltpu.VMEM((2, SEG, SUBLANES, D_MODEL), _F32),
            pltpu.SemaphoreType.DMA((H_SLOTS,)),
            pltpu.SemaphoreType.DMA((2,)),
        ],
        compiler_params=pltpu.CompilerParams(
            dimension_semantics=("arbitrary",), vmem_limit_bytes=VMEM_LIMIT),
        name="lru_layer",
    )(h, norm_pre, norm_post, w_in, conv_w, conv_b, w_a, b_a, w_x, b_x, lam, w_out, *next_weights)


def kernel(x, norm_pre, norm_post, attn_w_in, attn_w_out, attn_sinks, lru_w_in, lru_conv_w,
           lru_conv_b, lru_w_a, lru_b_a, lru_w_x, lru_b_x, lru_lambda, lru_w_out):
    depth = norm_pre.shape[0]
    row = lambda v: v.reshape(v.shape[0], 1, D_MODEL)
    norm_pre, norm_post = row(norm_pre), row(norm_post)
    lru_small = (lru_conv_w, row(lru_conv_b), row(lru_b_a), row(lru_b_x), row(lru_lambda))
    rows = lambda w: w.reshape(w.shape[0], D_MODEL, -1)
    attn_f32 = (attn_w_in, attn_w_out)
    lru_f32 = (lru_w_in, rows(lru_w_a), rows(lru_w_x), lru_w_out)
    weights = tuple(w[0].astype(_BF16) for w in attn_f32)
    h = x.reshape(SEQ, D_MODEL)
    for layer in range(depth):
        j = layer // 2
        last = layer == depth - 1
        if layer % 2 == 0:
            h, *weights = _attn_layer(h, layer, j, norm_pre, norm_post, attn_sinks, *weights,
                                      () if last else lru_f32, j)
        else:
            h, *weights = _lru_layer(h, layer, j, norm_pre, norm_post, lru_small, weights,
                                     () if last else attn_f32, j + 1)
    return h.reshape(x.shape)
```

```python
import functools

import jax
import jax.numpy as jnp
from jax import lax
from jax.experimental import pallas as pl
from jax.experimental.pallas import tpu as pltpu

D_MODEL = 2048
SEQ = 8192
HEAD_DIM = 64
N_HEADS = 32
N_KV_HEADS = 4
GROUP = N_HEADS // N_KV_HEADS
BLOCK = 128
KV_W = N_KV_HEADS * HEAD_DIM
GATE_COL = D_MODEL + 2 * KV_W
LRU_BLOCK_W = 256
N_LRU_BLOCKS = D_MODEL // LRU_BLOCK_W
CONV_W = 4
C_RG = 8.0
NORM_EPS = 1e-6
MASK_VALUE = -1e30
ATTN_SCALE = HEAD_DIM ** -0.5

LANES = 128
SUBLANES = 8
N_PAIRS = N_HEADS // 2
PAIRS_PER_KV = GROUP // 2

TM = 256
N_TILES = SEQ // TM
VMEM_LIMIT = 60 * 1024 * 1024

_F32 = jnp.float32
_BF16 = jnp.bfloat16


def _rms(x, g):
    ms = jnp.mean(x * x, axis=-1, keepdims=True)
    return x * lax.rsqrt(ms + NORM_EPS) * g


def _dot(a, b):
    return jnp.dot(a, b, preferred_element_type=_F32)


def _dot_nt(a, b):
    return lax.dot_general(a, b, (((1,), (1,)), ((), ())), preferred_element_type=_F32)


CAST_ROWS = D_MODEL // N_TILES


def _cast_rows(srcs, dsts):
    for src, dst in zip(srcs, dsts):
        dst[...] = src[...].astype(_BF16)


def _cast_specs(stacked, j):
    rows = lambda i: jnp.minimum(i, N_TILES - 1)
    in_specs = [pl.BlockSpec((None, CAST_ROWS, w.shape[-1]), lambda i: (j, rows(i), 0)) for w in stacked]
    out_specs = [pl.BlockSpec((CAST_ROWS, w.shape[-1]), lambda i: (rows(i), 0)) for w in stacked]
    out_shapes = [jax.ShapeDtypeStruct((D_MODEL, w.shape[-1]), _BF16) for w in stacked]
    return in_specs, out_specs, out_shapes


def _resident(arr):
    zeros = (0,) * arr.ndim
    return pl.BlockSpec(arr.shape, lambda i: zeros, pipeline_mode=pl.Buffered(1))


H_SLOTS = 3
O_SLOTS = 2


def _fetch_tile(step, in_copies):
    slot = step % H_SLOTS
    for c in in_copies(step, slot):
        c.wait()

    @pl.when(step + 1 < N_TILES)
    def _():
        for c in in_copies(step + 1, (step + 1) % H_SLOTS):
            c.start()
    return slot


def _write_back(step, out_copies, store, last):
    slot = (step - 1) % O_SLOTS

    @pl.when(step >= 1 + O_SLOTS)
    def _():
        for c in out_copies(step - 1 - O_SLOTS, slot):
            c.wait()
    store(slot)
    for c in out_copies(step - 1, slot):
        c.start()
    if last:
        for c in out_copies(step - 2, 1 - slot) + out_copies(step - 1, slot):
            c.wait()


def _staggered(step, body):
    pl.when(step == 0)(lambda: body(True, False))
    pl.when(jnp.logical_and(step > 0, step < N_TILES))(lambda: body(True, True))
    pl.when(step == N_TILES)(lambda: body(False, True))


def _attn_kernel(j, n_cast, sink_ref, h_hbm, gpre_ref, gpost_ref, win_ref, wout_ref, *refs):
    cast_src, (out_hbm, *cast_dst), scratch = refs[:n_cast], refs[n_cast:2 * n_cast + 1], refs[2 * n_cast + 1:]
    (u_ref, q3_ref, sg3_ref, y3_ref, klo_ref, khi_ref, vt_ref,
     h_buf, o_buf, in_sem, out_sem) = scratch
    _cast_rows(cast_src, cast_dst)
    step = pl.program_id(0)
    cur = step % 2
    prv = 1 - cur

    def in_copies(tile, slot):
        return [pltpu.make_async_copy(h_hbm.at[pl.ds(tile * TM, TM), :], h_buf.at[slot],
                                      in_sem.at[slot])]

    def out_copies(tile, slot):
        return [pltpu.make_async_copy(o_buf.at[slot], out_hbm.at[pl.ds(tile * TM, TM), :],
                                      out_sem.at[slot])]

    @pl.when(step == 0)
    def _():
        for c in in_copies(0, 0):
            c.start()

    nb = TM // BLOCK
    stack = PAIRS_PER_KV * BLOCK
    chunks = BLOCK // SUBLANES
    per_kv = GROUP * HEAD_DIM

    def project_kv(first):
        kv = _dot(u_ref[...], win_ref[:, D_MODEL:GATE_COL])
        lane_lo = lax.broadcasted_iota(jnp.int32, (TM, LANES), 1) < HEAD_DIM
        new = slice(BLOCK, BLOCK + TM)
        for c in range(N_KV_HEADS // 2):
            orig = kv[:, c * LANES:(c + 1) * LANES]
            sw = pltpu.roll(orig, HEAD_DIM, 1)
            klo_ref[cur, 2 * c, new, :] = jnp.where(lane_lo, orig, 0.0).astype(_BF16)
            khi_ref[cur, 2 * c, new, :] = jnp.where(lane_lo, 0.0, sw).astype(_BF16)
            klo_ref[cur, 2 * c + 1, new, :] = jnp.where(lane_lo, sw, 0.0).astype(_BF16)
            khi_ref[cur, 2 * c + 1, new, :] = jnp.where(lane_lo, 0.0, orig).astype(_BF16)
        vt_ref[cur, :, new] = kv[:, KV_W:2 * KV_W].T.astype(_BF16)
        for ref in (klo_ref, khi_ref):
            prev = jnp.zeros((N_KV_HEADS, BLOCK, LANES), _BF16) if first else ref[prv, :, TM:TM + BLOCK, :]
            ref[cur, :, 0:BLOCK, :] = prev
        vt_ref[cur, :, 0:BLOCK] = (jnp.zeros((KV_W, BLOCK), _BF16) if first
                                   else vt_ref[prv, :, TM:TM + BLOCK])

    def reduce_rows(x3, op):
        r = x3[0]
        for c in range(1, chunks):
            r = op(r, x3[c])
        for d in (1, 2, 4):
            r = op(r, pltpu.roll(r, d, 0))
        return r

    def softmax_band(s_full, sink_row, b, tri, prev_exists):
        s_prev = s_full[0:BLOCK]
        s_cur = s_full[BLOCK:2 * BLOCK]
        if b == 0:
            s_prev = jnp.where(prev_exists, s_prev, MASK_VALUE)
        s3 = jnp.where(tri, s_cur, s_prev).reshape(chunks, SUBLANES, stack)
        m = jnp.maximum(reduce_rows(s3, jnp.maximum), sink_row)
        e3 = jnp.exp(s3 - m[None])
        den = reduce_rows(e3, jnp.add) + jnp.exp(sink_row - m)
        pn = (e3 * (1.0 / den)[None]).reshape(BLOCK, stack)
        band = jnp.concatenate([jnp.where(tri, 0.0, pn), jnp.where(tri, pn, 0.0)], axis=0)
        return band.astype(_BF16)

    def sink_row(kvh, odd):
        cols = [jnp.full((SUBLANES, BLOCK), sink_ref[j, 2 * (PAIRS_PER_KV * kvh + i) + odd], _F32)
                for i in range(PAIRS_PER_KV)]
        return jnp.concatenate(cols, axis=1)

    def body(project, attend):
        if project:
            slot = _fetch_tile(step, in_copies)
            u_ref[...] = _rms(h_buf[slot], gpre_ref[...]).astype(_BF16)
        if attend:
            kj = lax.broadcasted_iota(jnp.int32, (BLOCK, stack), 0)
            qi = lax.broadcasted_iota(jnp.int32, (BLOCK, stack), 1) & (BLOCK - 1)
            tri = kj <= qi
            prev_exists = (kj + ((step - 1) * TM - BLOCK)) >= 0

        for kvh in range(N_KV_HEADS):
            pairs = slice(PAIRS_PER_KV * kvh, PAIRS_PER_KV * (kvh + 1))
            if attend:
                scores = []
                for b in range(nb):
                    rows = slice(b * BLOCK, (b + 1) * BLOCK)
                    band = slice(b * BLOCK, (b + 2) * BLOCK)
                    qs = q3_ref[prv, pairs, rows, :].reshape(stack, LANES)
                    scores.append((_dot_nt(klo_ref[prv, kvh, band, :], qs),
                                   _dot_nt(khi_ref[prv, kvh, band, :], qs)))

            if project:
                q = _dot(u_ref[...], win_ref[:, kvh * per_kv:(kvh + 1) * per_kv])
                gate = _dot(u_ref[...],
                            win_ref[:, GATE_COL + kvh * per_kv:GATE_COL + (kvh + 1) * per_kv])
                for i in range(PAIRS_PER_KV):
                    p = PAIRS_PER_KV * kvh + i
                    q3_ref[cur, p] = (q[:, i * LANES:(i + 1) * LANES] * ATTN_SCALE).astype(_BF16)
                    g = gate[:, i * LANES:(i + 1) * LANES]
                    sg3_ref[cur, p] = g * jax.nn.sigmoid(g)

            if attend:
                sink_e = sink_row(kvh, 0)
                sink_o = sink_row(kvh, 1)
                for b in range(nb):
                    rows = slice(b * BLOCK, (b + 1) * BLOCK)
                    band = slice(b * BLOCK, (b + 2) * BLOCK)
                    vt = vt_ref[prv, kvh * HEAD_DIM:(kvh + 1) * HEAD_DIM, band]
                    o_e = _dot(vt, softmax_band(scores[b][0], sink_e, b, tri, prev_exists))
                    o_o = _dot(vt, softmax_band(scores[b][1], sink_o, b, tri, prev_exists))
                    o2 = jnp.concatenate([o_e, o_o], axis=0).T
                    sg = sg3_ref[prv, pairs, rows, :].reshape(stack, LANES)
                    y3_ref[pairs, rows, :] = (o2 * sg).astype(_BF16).reshape(PAIRS_PER_KV, BLOCK, LANES)

        if project:
            project_kv(first=not attend)

        if attend:
            y = jnp.concatenate([y3_ref[p] for p in range(N_PAIRS)], axis=1)
            yo = _dot(y, wout_ref[...])

            def store(oslot):
                o_buf[oslot] = h_buf[(step - 1) % H_SLOTS] + _rms(yo, gpost_ref[...])
            _write_back(step, out_copies, store, last=not project)

    _staggered(step, body)


def _layer_spec(stacked, j):
    zeros = (0,) * (stacked.ndim - 1)
    return pl.BlockSpec((None,) + stacked.shape[1:], lambda i: (j,) + zeros,
                        pipeline_mode=pl.Buffered(1))


def _attn_layer(h, layer, j, norm_pre, norm_post, sinks, w_in, w_out, next_weights, j_next):
    band_rows = BLOCK + TM
    cast_in, cast_out, cast_shapes = _cast_specs(next_weights, j_next)
    return pl.pallas_call(
        functools.partial(_attn_kernel, j, len(next_weights)),
        out_shape=[jax.ShapeDtypeStruct((SEQ, D_MODEL), _F32)] + cast_shapes,
        grid=(N_TILES + 1,),
        in_specs=[
            pl.BlockSpec(memory_space=pltpu.SMEM),
            pl.BlockSpec(memory_space=pl.ANY),
            _layer_spec(norm_pre, layer),
            _layer_spec(norm_post, layer),
            _resident(w_in),
            _resident(w_out),
        ] + cast_in,
        out_specs=[pl.BlockSpec(memory_space=pl.ANY)] + cast_out,
        scratch_shapes=[
            pltpu.VMEM((TM, D_MODEL), _BF16),
            pltpu.VMEM((2, N_PAIRS, TM, LANES), _BF16),
            pltpu.VMEM((2, N_PAIRS, TM, LANES), _F32),
            pltpu.VMEM((N_PAIRS, TM, LANES), _BF16),
            pltpu.VMEM((2, N_KV_HEADS, band_rows, LANES), _BF16),
            pltpu.VMEM((2, N_KV_HEADS, band_rows, LANES), _BF16),
            pltpu.VMEM((2, KV_W, band_rows), _BF16),
            pltpu.VMEM((H_SLOTS, TM, D_MODEL), _F32),
            pltpu.VMEM((O_SLOTS, TM, D_MODEL), _F32),
            pltpu.SemaphoreType.DMA((H_SLOTS,)),
            pltpu.SemaphoreType.DMA((O_SLOTS,)),
        ],
        compiler_params=pltpu.CompilerParams(
            dimension_semantics=("arbitrary",), vmem_limit_bytes=VMEM_LIMIT),
        name="attn_layer",
    )(sinks, h, norm_pre, norm_post, w_in, w_out, *next_weights)


SEG = TM // SUBLANES
CONV_TAIL = (CONV_W - 1) * SUBLANES


def _sublane_scan(a, b, row):
    for d in (1, 2, 4):
        keep = row >= d
        a_sh = jnp.where(keep, pltpu.roll(a, d, 0), 1.0)
        b_sh = jnp.where(keep, pltpu.roll(b, d, 0), 0.0)
        b = a * b_sh + b
        a = a * a_sh
    return a, b


def _lru_kernel(n_cast, h_hbm, gpre_ref, gpost_ref, win_ref, cw_ref, cb_ref, wa_ref,
                ba_ref, wx_ref, bx_ref, lam_ref, wout_ref, *refs):
    cast_src, (out_hbm, *cast_dst), scratch = refs[:n_cast], refs[n_cast:2 * n_cast + 1], refs[2 * n_cast + 1:]
    (u_ref, x_ref, tail_ref, y_ref, yo_ref, sp_ref, hc_ref,
     h_buf, o_buf, in_sem, out_sem) = scratch
    _cast_rows(cast_src, cast_dst)
    step = pl.program_id(0)
    cur = step % 2
    prv = 1 - cur

    def in_copies(tile, slot):
        return [pltpu.make_async_copy(h_hbm.at[pl.ds(tile * TM + s * SEG, SEG), :],
                                      h_buf.at[slot, :, s, :], in_sem.at[slot])
                for s in range(SUBLANES)]

    def out_copies(tile, slot):
        return [pltpu.make_async_copy(o_buf.at[slot, :, s, :],
                                      out_hbm.at[pl.ds(tile * TM + s * SEG, SEG), :], out_sem.at[slot])
                for s in range(SUBLANES)]

    @pl.when(step == 0)
    def _():
        for c in in_copies(0, 0):
            c.start()
        tail_ref[...] = jnp.zeros(tail_ref.shape, _F32)
        hc_ref[...] = jnp.zeros(hc_ref.shape, _F32)
        nl = -lam_ref[...]
        softplus = jnp.maximum(nl, 0.0) + jnp.log1p(jnp.exp(-jnp.abs(nl)))
        sp_ref[...] = jnp.broadcast_to(softplus, sp_ref.shape)

    def block_cols(blk):
        return slice(blk * LRU_BLOCK_W, (blk + 1) * LRU_BLOCK_W)

    def project_x(blk):
        cs = block_cols(blk)
        x_ref[:, cs] = _dot(u_ref[...], win_ref[:, cs])

    def out_project(blk):
        cs = block_cols(blk)
        yo_ref[:, cs] = _dot(y_ref[prv], wout_ref[:, cs])

    def recur_block(blk, row, first_segment):
        cs = block_cols(blk)
        gate = _dot(u_ref[...],
                    win_ref[:, D_MODEL + blk * LRU_BLOCK_W:D_MODEL + (blk + 1) * LRU_BLOCK_W])

        x = x_ref[:, cs]
        tail_cur = x[TM - CONV_TAIL:]
        tail_prev = tail_ref[:, cs]
        tail_ref[:, cs] = tail_cur
        wrap = []
        for i in range(CONV_W - 1):
            rows = slice(i * SUBLANES, (i + 1) * SUBLANES)
            wrap.append(jnp.where(first_segment, pltpu.roll(tail_prev[rows], 1, 0),
                                  pltpu.roll(tail_cur[rows], 1, 0)))
        xc = cb_ref[0:1, cs] + cw_ref[CONV_W - 1:CONV_W, cs] * x
        for shift in range(1, CONV_W):
            tap = CONV_W - 1 - shift
            shifted = jnp.concatenate(wrap[CONV_W - 1 - shift:] + [x[:TM - shift * SUBLANES]], axis=0)
            xc = xc + cw_ref[tap:tap + 1, cs] * shifted
        xcb = xc.astype(_BF16)
        r = jax.nn.sigmoid(_dot(xcb, wa_ref[blk]) + ba_ref[0:1, cs])
        ig = jax.nn.sigmoid(_dot(xcb, wx_ref[blk]) + bx_ref[0:1, cs])
        sg = gate * jax.nn.sigmoid(gate)

        nla = (C_RG * r) * sp_ref[0:1, cs]
        a = jnp.exp(-nla)
        z = jnp.tanh(nla) * (a * a + 1.0)
        mult = jnp.where(z > 0.0, z * lax.rsqrt(z), 0.0)
        b = mult * (ig * xc)

        acc_a = a[0:SUBLANES]
        acc_h = b[0:SUBLANES]
        prods, local = [acc_a], [acc_h]
        for g in range(1, SEG):
            rows = slice(g * SUBLANES, (g + 1) * SUBLANES)
            acc_h = a[rows] * acc_h + b[rows]
            acc_a = a[rows] * acc_a
            prods.append(acc_a)
            local.append(acc_h)
        carry_in = hc_ref[:, cs]
        seg_a, seg_h = _sublane_scan(acc_a, acc_h, row)
        seg_end = seg_h + seg_a * carry_in
        entering = jnp.where(first_segment, carry_in, pltpu.roll(seg_end, 1, 0))
        hc_ref[:, cs] = jnp.broadcast_to(seg_end[SUBLANES - 1:SUBLANES, :], (SUBLANES, LRU_BLOCK_W))
        ys = [(local[g] + prods[g] * entering) * sg[g * SUBLANES:(g + 1) * SUBLANES]
              for g in range(SEG)]
        y_ref[cur, :, cs] = jnp.concatenate(ys, axis=0).astype(_BF16)

    def body(recur, finish):
        if recur:
            slot = _fetch_tile(step, in_copies)
        if finish:
            out_project(0)
            out_project(1)
        if recur:
            row = lax.broadcasted_iota(jnp.int32, (SUBLANES, LRU_BLOCK_W), 0)
            first_segment = row == 0
            u_ref[...] = _rms(h_buf[slot].reshape(TM, D_MODEL), gpre_ref[...]).astype(_BF16)
            project_x(0)
        for blk in range(N_LRU_BLOCKS):
            if recur and blk + 1 < N_LRU_BLOCKS:
                project_x(blk + 1)
            if finish and blk + 2 < N_LRU_BLOCKS:
                out_project(blk + 2)
            if recur:
                recur_block(blk, row, first_segment)
        if finish:
            def store(oslot):
                h_prev = h_buf[(step - 1) % H_SLOTS].reshape(TM, D_MODEL)
                out = h_prev + _rms(yo_ref[...], gpost_ref[...])
                o_buf[oslot] = out.reshape(SEG, SUBLANES, D_MODEL)
            _write_back(step, out_copies, store, last=not recur)

    _staggered(step, body)


def _lru_layer(h, layer, j, norm_pre, norm_post, small, weights, next_weights, j_next):
    conv_w, conv_b, b_a, b_x, lam = small
    w_in, w_a, w_x, w_out = weights
    gate_shape = (N_LRU_BLOCKS, LRU_BLOCK_W, LRU_BLOCK_W)
    w_a, w_x = w_a.reshape(gate_shape), w_x.reshape(gate_shape)
    cast_in, cast_out, cast_shapes = _cast_specs(next_weights, j_next)
    return pl.pallas_call(
        functools.partial(_lru_kernel, len(next_weights)),
        out_shape=[jax.ShapeDtypeStruct((SEQ, D_MODEL), _F32)] + cast_shapes,
        grid=(N_TILES + 1,),
        in_specs=[
            pl.BlockSpec(memory_space=pl.ANY),
            _layer_spec(norm_pre, layer),
            _layer_spec(norm_post, layer),
            _resident(w_in), _layer_spec(conv_w, j), _layer_spec(conv_b, j),
            _resident(w_a), _layer_spec(b_a, j), _resident(w_x), _layer_spec(b_x, j),
            _layer_spec(lam, j), _resident(w_out),
        ] + cast_in,
        out_specs=[pl.BlockSpec(memory_space=pl.ANY)] + cast_out,
        scratch_shapes=[
            pltpu.VMEM((TM, D_MODEL), _BF16),
            pltpu.VMEM((TM, D_MODEL), _F32),
            pltpu.VMEM((CONV_TAIL, D_MODEL), _F32),
            pltpu.VMEM((2, TM, D_MODEL), _BF16),
            pltpu.VMEM((TM, D_MODEL), _F32),
            pltpu.VMEM((SUBLANES, D_MODEL), _F32),
            pltpu.VMEM((SUBLANES, D_MODEL), _F32),
            pltpu.VMEM((H_SLOTS, SEG, SUBLANES, D_MODEL), _F32),
            pltpu.VMEM((O_SLOTS, SEG, SUBLANES, D_MODEL), _F32),
            pltpu.SemaphoreType.DMA((H_SLOTS,)),
            pltpu.SemaphoreType.DMA((O_SLOTS,)),
        ],
        compiler_params=pltpu.CompilerParams(
            dimension_semantics=("arbitrary",), vmem_limit_bytes=VMEM_LIMIT),
        name="lru_layer",
    )(h, norm_pre, norm_post, w_in, conv_w, conv_b, w_a, b_a, w_x, b_x, lam, w_out, *next_weights)


def kernel(x, norm_pre, norm_post, attn_w_in, attn_w_out, attn_sinks, lru_w_in, lru_conv_w,
           lru_conv_b, lru_w_a, lru_b_a, lru_w_x, lru_b_x, lru_lambda, lru_w_out):
    depth = norm_pre.shape[0]
    row = lambda v: v.reshape(v.shape[0], 1, D_MODEL)
    norm_pre, norm_post = row(norm_pre), row(norm_post)
    lru_small = (lru_conv_w, row(lru_conv_b), row(lru_b_a), row(lru_b_x), row(lru_lambda))
    rows = lambda w: w.reshape(w.shape[0], D_MODEL, -1)
    attn_f32 = (attn_w_in, attn_w_out)
    lru_f32 = (lru_w_in, rows(lru_w_a), rows(lru_w_x), lru_w_out)
    weights = tuple(w[0].astype(_BF16) for w in attn_f32)
    h = x.reshape(SEQ, D_MODEL)
    for layer in range(depth):
        j = layer // 2
        last = layer == depth - 1
        if layer % 2 == 0:
            h, *weights = _attn_layer(h, layer, j, norm_pre, norm_post, attn_sinks, *weights,
                                      () if last else lru_f32, j)
        else:
            h, *weights = _lru_layer(h, layer, j, norm_pre, norm_post, lru_small, weights,
                                     () if last else attn_f32, j + 1)
    return h.reshape(x.shape)
```

```python
import functools

import jax
import jax.numpy as jnp
from jax import lax
from jax.experimental import pallas as pl
from jax.experimental.pallas import tpu as pltpu

D_MODEL = 2048
SEQ = 8192
HEAD_DIM = 64
N_HEADS = 32
N_KV_HEADS = 4
GROUP = N_HEADS // N_KV_HEADS
BLOCK = 128
KV_W = N_KV_HEADS * HEAD_DIM
GATE_COL = D_MODEL + 2 * KV_W
LRU_BLOCK_W = 256
N_LRU_BLOCKS = D_MODEL // LRU_BLOCK_W
CONV_W = 4
C_RG = 8.0
NORM_EPS = 1e-6
MASK_VALUE = -1e30
ATTN_SCALE = HEAD_DIM ** -0.5

LANES = 128
SUBLANES = 8
N_PAIRS = N_HEADS // 2
PAIRS_PER_KV = GROUP // 2

TM = 256
N_TILES = SEQ // TM
VMEM_LIMIT = 60 * 1024 * 1024

_F32 = jnp.float32
_BF16 = jnp.bfloat16


def _rms(x, g):
    ms = jnp.mean(x * x, axis=-1, keepdims=True)
    return x * lax.rsqrt(ms + NORM_EPS) * g


def _dot(a, b):
    return jnp.dot(a, b, preferred_element_type=_F32)


def _dot_nt(a, b):
    return lax.dot_general(a, b, (((1,), (1,)), ((), ())), preferred_element_type=_F32)


CAST_ROWS = D_MODEL // N_TILES


def _cast_rows(srcs, dsts):
    for src, dst in zip(srcs, dsts):
        dst[...] = src[...].astype(_BF16)


def _cast_specs(casts):
    rows = lambda i: jnp.minimum(i, N_TILES - 1)
    in_specs = [pl.BlockSpec((None, CAST_ROWS, w.shape[-1]), lambda i, j=j: (j, rows(i), 0))
                for w, j in casts]
    out_specs = [pl.BlockSpec((CAST_ROWS, w.shape[-1]), lambda i: (rows(i), 0)) for w, _ in casts]
    out_shapes = [jax.ShapeDtypeStruct((D_MODEL, w.shape[-1]), _BF16) for w, _ in casts]
    return in_specs, out_specs, out_shapes


def _resident(arr):
    zeros = (0,) * arr.ndim
    return pl.BlockSpec(arr.shape, lambda i: zeros, pipeline_mode=pl.Buffered(1))


H_SLOTS = 3
O_SLOTS = 2


def _fetch_tile(step, in_copies):
    slot = step % H_SLOTS
    for c in in_copies(step, slot):
        c.wait()

    @pl.when(step + 1 < N_TILES)
    def _():
        for c in in_copies(step + 1, (step + 1) % H_SLOTS):
            c.start()
    return slot


def _write_back(step, out_copies, store, last):
    slot = (step - 1) % O_SLOTS

    @pl.when(step >= 1 + O_SLOTS)
    def _():
        for c in out_copies(step - 1 - O_SLOTS, slot):
            c.wait()
    store(slot)
    for c in out_copies(step - 1, slot):
        c.start()
    if last:
        for c in out_copies(step - 2, 1 - slot) + out_copies(step - 1, slot):
            c.wait()


def _staggered(step, body):
    pl.when(step == 0)(lambda: body(True, False))
    pl.when(jnp.logical_and(step > 0, step < N_TILES))(lambda: body(True, True))
    pl.when(step == N_TILES)(lambda: body(False, True))


def _attn_kernel(j, n_cast, sink_ref, h_hbm, gpre_ref, gpost_ref, win_ref, wout_ref, *refs):
    cast_src, (out_hbm, *cast_dst), scratch = refs[:n_cast], refs[n_cast:2 * n_cast + 1], refs[2 * n_cast + 1:]
    (u_ref, q3_ref, sg3_ref, y3_ref, klo_ref, khi_ref, vt_ref,
     h_buf, o_buf, in_sem, out_sem) = scratch
    _cast_rows(cast_src, cast_dst)
    step = pl.program_id(0)
    cur = step % 2
    prv = 1 - cur

    def in_copies(tile, slot):
        return [pltpu.make_async_copy(h_hbm.at[pl.ds(tile * TM, TM), :], h_buf.at[slot],
                                      in_sem.at[slot])]

    def out_copies(tile, slot):
        return [pltpu.make_async_copy(o_buf.at[slot], out_hbm.at[pl.ds(tile * TM, TM), :],
                                      out_sem.at[slot])]

    @pl.when(step == 0)
    def _():
        for c in in_copies(0, 0):
            c.start()

    nb = TM // BLOCK
    stack = PAIRS_PER_KV * BLOCK
    chunks = BLOCK // SUBLANES
    per_kv = GROUP * HEAD_DIM

    def project_kv(first):
        kv = _dot(u_ref[...], win_ref[:, D_MODEL:GATE_COL])
        lane_lo = lax.broadcasted_iota(jnp.int32, (TM, LANES), 1) < HEAD_DIM
        new = slice(BLOCK, BLOCK + TM)
        for c in range(N_KV_HEADS // 2):
            orig = kv[:, c * LANES:(c + 1) * LANES]
            sw = pltpu.roll(orig, HEAD_DIM, 1)
            klo_ref[cur, 2 * c, new, :] = jnp.where(lane_lo, orig, 0.0).astype(_BF16)
            khi_ref[cur, 2 * c, new, :] = jnp.where(lane_lo, 0.0, sw).astype(_BF16)
            klo_ref[cur, 2 * c + 1, new, :] = jnp.where(lane_lo, sw, 0.0).astype(_BF16)
            khi_ref[cur, 2 * c + 1, new, :] = jnp.where(lane_lo, 0.0, orig).astype(_BF16)
        vt_ref[cur, :, new] = kv[:, KV_W:2 * KV_W].T.astype(_BF16)
        for ref in (klo_ref, khi_ref):
            prev = jnp.zeros((N_KV_HEADS, BLOCK, LANES), _BF16) if first else ref[prv, :, TM:TM + BLOCK, :]
            ref[cur, :, 0:BLOCK, :] = prev
        vt_ref[cur, :, 0:BLOCK] = (jnp.zeros((KV_W, BLOCK), _BF16) if first
                                   else vt_ref[prv, :, TM:TM + BLOCK])

    def reduce_rows(x3, op):
        r = x3[0]
        for c in range(1, chunks):
            r = op(r, x3[c])
        for d in (1, 2, 4):
            r = op(r, pltpu.roll(r, d, 0))
        return r

    def softmax_band(s_full, sink_row, b, tri, prev_exists):
        s_prev = s_full[0:BLOCK]
        s_cur = s_full[BLOCK:2 * BLOCK]
        if b == 0:
            s_prev = jnp.where(prev_exists, s_prev, MASK_VALUE)
        s3 = jnp.where(tri, s_cur, s_prev).reshape(chunks, SUBLANES, stack)
        m = jnp.maximum(reduce_rows(s3, jnp.maximum), sink_row)
        e3 = jnp.exp(s3 - m[None])
        den = reduce_rows(e3, jnp.add) + jnp.exp(sink_row - m)
        pn = (e3 * (1.0 / den)[None]).reshape(BLOCK, stack)
        band = jnp.concatenate([jnp.where(tri, 0.0, pn), jnp.where(tri, pn, 0.0)], axis=0)
        return band.astype(_BF16)

    def sink_row(kvh, odd):
        cols = [jnp.full((SUBLANES, BLOCK), sink_ref[j, 2 * (PAIRS_PER_KV * kvh + i) + odd], _F32)
                for i in range(PAIRS_PER_KV)]
        return jnp.concatenate(cols, axis=1)

    def body(project, attend):
        if project:
            slot = _fetch_tile(step, in_copies)
            u_ref[...] = _rms(h_buf[slot], gpre_ref[...]).astype(_BF16)
        if attend:
            kj = lax.broadcasted_iota(jnp.int32, (BLOCK, stack), 0)
            qi = lax.broadcasted_iota(jnp.int32, (BLOCK, stack), 1) & (BLOCK - 1)
            tri = kj <= qi
            prev_exists = (kj + ((step - 1) * TM - BLOCK)) >= 0

        for kvh in range(N_KV_HEADS):
            pairs = slice(PAIRS_PER_KV * kvh, PAIRS_PER_KV * (kvh + 1))
            if attend:
                scores = []
                for b in range(nb):
                    rows = slice(b * BLOCK, (b + 1) * BLOCK)
                    band = slice(b * BLOCK, (b + 2) * BLOCK)
                    qs = q3_ref[prv, pairs, rows, :].reshape(stack, LANES)
                    scores.append((_dot_nt(klo_ref[prv, kvh, band, :], qs),
                                   _dot_nt(khi_ref[prv, kvh, band, :], qs)))

            if project:
                q = _dot(u_ref[...], win_ref[:, kvh * per_kv:(kvh + 1) * per_kv])
                gate = _dot(u_ref[...],
                            win_ref[:, GATE_COL + kvh * per_kv:GATE_COL + (kvh + 1) * per_kv])
                for i in range(PAIRS_PER_KV):
                    p = PAIRS_PER_KV * kvh + i
                    q3_ref[cur, p] = (q[:, i * LANES:(i + 1) * LANES] * ATTN_SCALE).astype(_BF16)
                    g = gate[:, i * LANES:(i + 1) * LANES]
                    sg3_ref[cur, p] = g * jax.nn.sigmoid(g)

            if attend:
                sink_e = sink_row(kvh, 0)
                sink_o = sink_row(kvh, 1)
                for b in range(nb):
                    rows = slice(b * BLOCK, (b + 1) * BLOCK)
                    band = slice(b * BLOCK, (b + 2) * BLOCK)
                    vt = vt_ref[prv, kvh * HEAD_DIM:(kvh + 1) * HEAD_DIM, band]
                    o_e = _dot(vt, softmax_band(scores[b][0], sink_e, b, tri, prev_exists))
                    o_o = _dot(vt, softmax_band(scores[b][1], sink_o, b, tri, prev_exists))
                    o2 = jnp.concatenate([o_e, o_o], axis=0).T
                    sg = sg3_ref[prv, pairs, rows, :].reshape(stack, LANES)
                    y3_ref[pairs, rows, :] = (o2 * sg).astype(_BF16).reshape(PAIRS_PER_KV, BLOCK, LANES)

        if project:
            project_kv(first=not attend)

        if attend:
            y = jnp.concatenate([y3_ref[p] for p in range(N_PAIRS)], axis=1)
            yo = _dot(y, wout_ref[...])

            def store(oslot):
                o_buf[oslot] = h_buf[(step - 1) % H_SLOTS] + _rms(yo, gpost_ref[...])
            _write_back(step, out_copies, store, last=not project)

    _staggered(step, body)


def _layer_spec(stacked, j):
    zeros = (0,) * (stacked.ndim - 1)
    return pl.BlockSpec((None,) + stacked.shape[1:], lambda i: (j,) + zeros,
                        pipeline_mode=pl.Buffered(1))


def _attn_layer(h, layer, j, norm_pre, norm_post, sinks, w_in, w_out, casts):
    band_rows = BLOCK + TM
    cast_in, cast_out, cast_shapes = _cast_specs(casts)
    return pl.pallas_call(
        functools.partial(_attn_kernel, j, len(casts)),
        out_shape=[jax.ShapeDtypeStruct((SEQ, D_MODEL), _F32)] + cast_shapes,
        grid=(N_TILES + 1,),
        in_specs=[
            pl.BlockSpec(memory_space=pltpu.SMEM),
            pl.BlockSpec(memory_space=pl.ANY),
            _layer_spec(norm_pre, layer),
            _layer_spec(norm_post, layer),
            _resident(w_in),
            _resident(w_out),
        ] + cast_in,
        out_specs=[pl.BlockSpec(memory_space=pl.ANY)] + cast_out,
        scratch_shapes=[
            pltpu.VMEM((TM, D_MODEL), _BF16),
            pltpu.VMEM((2, N_PAIRS, TM, LANES), _BF16),
            pltpu.VMEM((2, N_PAIRS, TM, LANES), _F32),
            pltpu.VMEM((N_PAIRS, TM, LANES), _BF16),
            pltpu.VMEM((2, N_KV_HEADS, band_rows, LANES), _BF16),
            pltpu.VMEM((2, N_KV_HEADS, band_rows, LANES), _BF16),
            pltpu.VMEM((2, KV_W, band_rows), _BF16),
            pltpu.VMEM((H_SLOTS, TM, D_MODEL), _F32),
            pltpu.VMEM((O_SLOTS, TM, D_MODEL), _F32),
            pltpu.SemaphoreType.DMA((H_SLOTS,)),
            pltpu.SemaphoreType.DMA((O_SLOTS,)),
        ],
        compiler_params=pltpu.CompilerParams(
            dimension_semantics=("arbitrary",), vmem_limit_bytes=VMEM_LIMIT),
        name="attn_layer",
    )(sinks, h, norm_pre, norm_post, w_in, w_out, *[w for w, _ in casts])


SEG = TM // SUBLANES
CONV_TAIL = (CONV_W - 1) * SUBLANES


def _sublane_scan(a, b, row):
    for d in (1, 2, 4):
        keep = row >= d
        a_sh = jnp.where(keep, pltpu.roll(a, d, 0), 1.0)
        b_sh = jnp.where(keep, pltpu.roll(b, d, 0), 0.0)
        b = a * b_sh + b
        a = a * a_sh
    return a, b


def _lru_kernel(n_cast, h_hbm, gpre_ref, gpost_ref, win_ref, cw_ref, cb_ref, wa_ref,
                ba_ref, wx_ref, bx_ref, lam_ref, wout_ref, *refs):
    cast_src, (out_hbm, *cast_dst), scratch = refs[:n_cast], refs[n_cast:2 * n_cast + 1], refs[2 * n_cast + 1:]
    (u_ref, x_ref, tail_ref, y_ref, yo_ref, sp_ref, hc_ref,
     h_buf, o_buf, in_sem, out_sem) = scratch
    _cast_rows(cast_src, cast_dst)
    step = pl.program_id(0)
    cur = step % 2
    prv = 1 - cur

    def in_copies(tile, slot):
        return [pltpu.make_async_copy(h_hbm.at[pl.ds(tile * TM + s * SEG, SEG), :],
                                      h_buf.at[slot, :, s, :], in_sem.at[slot])
                for s in range(SUBLANES)]

    def out_copies(tile, slot):
        return [pltpu.make_async_copy(o_buf.at[slot, :, s, :],
                                      out_hbm.at[pl.ds(tile * TM + s * SEG, SEG), :], out_sem.at[slot])
                for s in range(SUBLANES)]

    @pl.when(step == 0)
    def _():
        for c in in_copies(0, 0):
            c.start()
        tail_ref[...] = jnp.zeros(tail_ref.shape, _F32)
        hc_ref[...] = jnp.zeros(hc_ref.shape, _F32)
        nl = -lam_ref[...]
        softplus = jnp.maximum(nl, 0.0) + jnp.log1p(jnp.exp(-jnp.abs(nl)))
        sp_ref[...] = jnp.broadcast_to(softplus, sp_ref.shape)

    def block_cols(blk):
        return slice(blk * LRU_BLOCK_W, (blk + 1) * LRU_BLOCK_W)

    def project_x(blk):
        cs = block_cols(blk)
        x_ref[:, cs] = _dot(u_ref[...], win_ref[:, cs])

    def out_project(blk):
        cs = block_cols(blk)
        yo_ref[:, cs] = _dot(y_ref[prv], wout_ref[:, cs])

    def recur_block(blk, row, first_segment):
        cs = block_cols(blk)
        gate = _dot(u_ref[...],
                    win_ref[:, D_MODEL + blk * LRU_BLOCK_W:D_MODEL + (blk + 1) * LRU_BLOCK_W])

        x = x_ref[:, cs]
        tail_cur = x[TM - CONV_TAIL:]
        tail_prev = tail_ref[:, cs]
        tail_ref[:, cs] = tail_cur
        wrap = []
        for i in range(CONV_W - 1):
            rows = slice(i * SUBLANES, (i + 1) * SUBLANES)
            wrap.append(jnp.where(first_segment, pltpu.roll(tail_prev[rows], 1, 0),
                                  pltpu.roll(tail_cur[rows], 1, 0)))
        xc = cb_ref[0:1, cs] + cw_ref[CONV_W - 1:CONV_W, cs] * x
        for shift in range(1, CONV_W):
            tap = CONV_W - 1 - shift
            shifted = jnp.concatenate(wrap[CONV_W - 1 - shift:] + [x[:TM - shift * SUBLANES]], axis=0)
            xc = xc + cw_ref[tap:tap + 1, cs] * shifted
        xcb = xc.astype(_BF16)
        r = jax.nn.sigmoid(_dot(xcb, wa_ref[blk]) + ba_ref[0:1, cs])
        ig = jax.nn.sigmoid(_dot(xcb, wx_ref[blk]) + bx_ref[0:1, cs])
        sg = gate * jax.nn.sigmoid(gate)

        nla = (C_RG * r) * sp_ref[0:1, cs]
        a = jnp.exp(-nla)
        z = jnp.tanh(nla) * (a * a + 1.0)
        mult = jnp.where(z > 0.0, z * lax.rsqrt(z), 0.0)
        b = mult * (ig * xc)

        acc_a = a[0:SUBLANES]
        acc_h = b[0:SUBLANES]
        prods, local = [acc_a], [acc_h]
        for g in range(1, SEG):
            rows = slice(g * SUBLANES, (g + 1) * SUBLANES)
            acc_h = a[rows] * acc_h + b[rows]
            acc_a = a[rows] * acc_a
            prods.append(acc_a)
            local.append(acc_h)
        carry_in = hc_ref[:, cs]
        seg_a, seg_h = _sublane_scan(acc_a, acc_h, row)
        seg_end = seg_h + seg_a * carry_in
        entering = jnp.where(first_segment, carry_in, pltpu.roll(seg_end, 1, 0))
        hc_ref[:, cs] = jnp.broadcast_to(seg_end[SUBLANES - 1:SUBLANES, :], (SUBLANES, LRU_BLOCK_W))
        ys = [(local[g] + prods[g] * entering) * sg[g * SUBLANES:(g + 1) * SUBLANES]
              for g in range(SEG)]
        y_ref[cur, :, cs] = jnp.concatenate(ys, axis=0).astype(_BF16)

    def body(recur, finish):
        if recur:
            slot = _fetch_tile(step, in_copies)
        if finish:
            out_project(0)
            out_project(1)
        if recur:
            row = lax.broadcasted_iota(jnp.int32, (SUBLANES, LRU_BLOCK_W), 0)
            first_segment = row == 0
            u_ref[...] = _rms(h_buf[slot].reshape(TM, D_MODEL), gpre_ref[...]).astype(_BF16)
            project_x(0)
        for blk in range(N_LRU_BLOCKS):
            if recur and blk + 1 < N_LRU_BLOCKS:
                project_x(blk + 1)
            if finish and blk + 2 < N_LRU_BLOCKS:
                out_project(blk + 2)
            if recur:
                recur_block(blk, row, first_segment)
        if finish:
            def store(oslot):
                h_prev = h_buf[(step - 1) % H_SLOTS].reshape(TM, D_MODEL)
                out = h_prev + _rms(yo_ref[...], gpost_ref[...])
                o_buf[oslot] = out.reshape(SEG, SUBLANES, D_MODEL)
            _write_back(step, out_copies, store, last=not recur)

    _staggered(step, body)


def _lru_layer(h, layer, j, norm_pre, norm_post, small, weights, casts):
    conv_w, conv_b, b_a, b_x, lam = small
    w_in, w_a, w_x, w_out = weights
    gate_shape = (N_LRU_BLOCKS, LRU_BLOCK_W, LRU_BLOCK_W)
    w_a, w_x = w_a.reshape(gate_shape), w_x.reshape(gate_shape)
    cast_in, cast_out, cast_shapes = _cast_specs(casts)
    return pl.pallas_call(
        functools.partial(_lru_kernel, len(casts)),
        out_shape=[jax.ShapeDtypeStruct((SEQ, D_MODEL), _F32)] + cast_shapes,
        grid=(N_TILES + 1,),
        in_specs=[
            pl.BlockSpec(memory_space=pl.ANY),
            _layer_spec(norm_pre, layer),
            _layer_spec(norm_post, layer),
            _resident(w_in), _layer_spec(conv_w, j), _layer_spec(conv_b, j),
            _resident(w_a), _layer_spec(b_a, j), _resident(w_x), _layer_spec(b_x, j),
            _layer_spec(lam, j), _resident(w_out),
        ] + cast_in,
        out_specs=[pl.BlockSpec(memory_space=pl.ANY)] + cast_out,
        scratch_shapes=[
            pltpu.VMEM((TM, D_MODEL), _BF16),
            pltpu.VMEM((TM, D_MODEL), _F32),
            pltpu.VMEM((CONV_TAIL, D_MODEL), _F32),
            pltpu.VMEM((2, TM, D_MODEL), _BF16),
            pltpu.VMEM((TM, D_MODEL), _F32),
            pltpu.VMEM((SUBLANES, D_MODEL), _F32),
            pltpu.VMEM((SUBLANES, D_MODEL), _F32),
            pltpu.VMEM((H_SLOTS, SEG, SUBLANES, D_MODEL), _F32),
            pltpu.VMEM((O_SLOTS, SEG, SUBLANES, D_MODEL), _F32),
            pltpu.SemaphoreType.DMA((H_SLOTS,)),
            pltpu.SemaphoreType.DMA((O_SLOTS,)),
        ],
        compiler_params=pltpu.CompilerParams(
            dimension_semantics=("arbitrary",), vmem_limit_bytes=VMEM_LIMIT),
        name="lru_layer",
    )(h, norm_pre, norm_post, w_in, conv_w, conv_b, w_a, b_a, w_x, b_x, lam, w_out,
      *[w for w, _ in casts])


def kernel(x, norm_pre, norm_post, attn_w_in, attn_w_out, attn_sinks, lru_w_in, lru_conv_w,
           lru_conv_b, lru_w_a, lru_b_a, lru_w_x, lru_b_x, lru_lambda, lru_w_out):
    depth = norm_pre.shape[0]
    row = lambda v: v.reshape(v.shape[0], 1, D_MODEL)
    norm_pre, norm_post = row(norm_pre), row(norm_post)
    lru_small = (lru_conv_w, row(lru_conv_b), row(lru_b_a), row(lru_b_x), row(lru_lambda))
    rows = lambda w: w.reshape(w.shape[0], D_MODEL, -1)
    attn_f32 = (attn_w_in, attn_w_out)
    lru_f32 = (lru_w_in, rows(lru_w_a), rows(lru_w_x), lru_w_out)
    weights = [w[0].astype(_BF16) for w in attn_f32]
    ahead = []
    h = x.reshape(SEQ, D_MODEL)
    for layer in range(depth):
        j = layer // 2
        if layer % 2 == 0:
            casts = [(w, j) for w in lru_f32] if layer + 1 < depth else []
            casts += [(w, j + 1) for w in attn_f32] if layer + 2 < depth else []
            h, *cast = _attn_layer(h, layer, j, norm_pre, norm_post, attn_sinks, *weights, casts)
            weights, ahead = cast[:len(lru_f32)], cast[len(lru_f32):]
        else:
            h, = _lru_layer(h, layer, j, norm_pre, norm_post, lru_small, weights, [])
            weights = ahead
    return h.reshape(x.shape)
```

```python
import functools

import jax
import jax.numpy as jnp
from jax import lax
from jax.experimental import pallas as pl
from jax.experimental.pallas import tpu as pltpu

D_MODEL = 2048
SEQ = 8192
HEAD_DIM = 64
N_HEADS = 32
N_KV_HEADS = 4
GROUP = N_HEADS // N_KV_HEADS
BLOCK = 128
KV_W = N_KV_HEADS * HEAD_DIM
GATE_COL = D_MODEL + 2 * KV_W
LRU_BLOCK_W = 256
N_LRU_BLOCKS = D_MODEL // LRU_BLOCK_W
CONV_W = 4
C_RG = 8.0
NORM_EPS = 1e-6
MASK_VALUE = -1e30
ATTN_SCALE = HEAD_DIM ** -0.5

LANES = 128
SUBLANES = 8
N_PAIRS = N_HEADS // 2
PAIRS_PER_KV = GROUP // 2

TM = 256
N_TILES = SEQ // TM
VMEM_LIMIT = 60 * 1024 * 1024

_F32 = jnp.float32
_BF16 = jnp.bfloat16


def _rms(x, g):
    ms = jnp.mean(x * x, axis=-1, keepdims=True)
    return x * lax.rsqrt(ms + NORM_EPS) * g


def _dot(a, b):
    return jnp.dot(a, b, preferred_element_type=_F32)


def _dot_nt(a, b):
    return lax.dot_general(a, b, (((1,), (1,)), ((), ())), preferred_element_type=_F32)


CAST_ROWS = D_MODEL // N_TILES


def _cast_rows(srcs, dsts):
    for src, dst in zip(srcs, dsts):
        dst[...] = src[...].astype(_BF16)


def _cast_specs(casts):
    rows = lambda i: jnp.minimum(i, N_TILES - 1)
    in_specs = [pl.BlockSpec((None, CAST_ROWS, w.shape[-1]), lambda i, j=j: (j, rows(i), 0))
                for w, j in casts]
    out_specs = [pl.BlockSpec((CAST_ROWS, w.shape[-1]), lambda i: (rows(i), 0)) for w, _ in casts]
    out_shapes = [jax.ShapeDtypeStruct((D_MODEL, w.shape[-1]), _BF16) for w, _ in casts]
    return in_specs, out_specs, out_shapes


def _resident(arr):
    zeros = (0,) * arr.ndim
    return pl.BlockSpec(arr.shape, lambda i: zeros, pipeline_mode=pl.Buffered(1))


H_SLOTS = 3
O_SLOTS = 2


def _fetch_tile(step, in_copies):
    slot = step % H_SLOTS
    for c in in_copies(step, slot):
        c.wait()

    @pl.when(step + 1 < N_TILES)
    def _():
        for c in in_copies(step + 1, (step + 1) % H_SLOTS):
            c.start()
    return slot


def _write_back(step, out_copies, store, last):
    slot = (step - 1) % O_SLOTS

    @pl.when(step >= 1 + O_SLOTS)
    def _():
        for c in out_copies(step - 1 - O_SLOTS, slot):
            c.wait()
    store(slot)
    for c in out_copies(step - 1, slot):
        c.start()
    if last:
        for c in out_copies(step - 2, 1 - slot) + out_copies(step - 1, slot):
            c.wait()


def _staggered(step, body):
    pl.when(step == 0)(lambda: body(True, False))
    pl.when(jnp.logical_and(step > 0, step < N_TILES))(lambda: body(True, True))
    pl.when(step == N_TILES)(lambda: body(False, True))


def _attn_kernel(layer, j, n_cast, sink_ref, h_hbm, gpre_ref, gpost_ref, win_ref, wout_ref, *refs):
    cast_src, (out_hbm, *cast_dst), scratch = refs[:n_cast], refs[n_cast:2 * n_cast + 1], refs[2 * n_cast + 1:]
    (u_ref, q3_ref, sg3_ref, y3_ref, klo_ref, khi_ref, vt_ref,
     h_buf, o_buf, in_sem, out_sem) = scratch
    _cast_rows(cast_src, cast_dst)
    step = pl.program_id(0)
    cur = step % 2
    prv = 1 - cur

    def in_copies(tile, slot):
        return [pltpu.make_async_copy(h_hbm.at[pl.ds(tile * TM, TM), :], h_buf.at[slot],
                                      in_sem.at[slot])]

    def out_copies(tile, slot):
        return [pltpu.make_async_copy(o_buf.at[slot], out_hbm.at[pl.ds(tile * TM, TM), :],
                                      out_sem.at[slot])]

    @pl.when(step == 0)
    def _():
        for c in in_copies(0, 0):
            c.start()

    nb = TM // BLOCK
    stack = PAIRS_PER_KV * BLOCK
    chunks = BLOCK // SUBLANES
    per_kv = GROUP * HEAD_DIM

    def project_kv(first):
        kv = _dot(u_ref[...], win_ref[:, D_MODEL:GATE_COL])
        lane_lo = lax.broadcasted_iota(jnp.int32, (TM, LANES), 1) < HEAD_DIM
        new = slice(BLOCK, BLOCK + TM)
        for c in range(N_KV_HEADS // 2):
            orig = kv[:, c * LANES:(c + 1) * LANES]
            sw = pltpu.roll(orig, HEAD_DIM, 1)
            klo_ref[cur, 2 * c, new, :] = jnp.where(lane_lo, orig, 0.0).astype(_BF16)
            khi_ref[cur, 2 * c, new, :] = jnp.where(lane_lo, 0.0, sw).astype(_BF16)
            klo_ref[cur, 2 * c + 1, new, :] = jnp.where(lane_lo, sw, 0.0).astype(_BF16)
            khi_ref[cur, 2 * c + 1, new, :] = jnp.where(lane_lo, 0.0, orig).astype(_BF16)
        vt_ref[cur, :, new] = kv[:, KV_W:2 * KV_W].T.astype(_BF16)
        for ref in (klo_ref, khi_ref):
            prev = jnp.zeros((N_KV_HEADS, BLOCK, LANES), _BF16) if first else ref[prv, :, TM:TM + BLOCK, :]
            ref[cur, :, 0:BLOCK, :] = prev
        vt_ref[cur, :, 0:BLOCK] = (jnp.zeros((KV_W, BLOCK), _BF16) if first
                                   else vt_ref[prv, :, TM:TM + BLOCK])

    def reduce_rows(x3, op):
        r = x3[0]
        for c in range(1, chunks):
            r = op(r, x3[c])
        for d in (1, 2, 4):
            r = op(r, pltpu.roll(r, d, 0))
        return r

    def softmax_band(s_full, sink_row, b, tri, prev_exists):
        s_prev = s_full[0:BLOCK]
        s_cur = s_full[BLOCK:2 * BLOCK]
        if b == 0:
            s_prev = jnp.where(prev_exists, s_prev, MASK_VALUE)
        s3 = jnp.where(tri, s_cur, s_prev).reshape(chunks, SUBLANES, stack)
        m = jnp.maximum(reduce_rows(s3, jnp.maximum), sink_row)
        e3 = jnp.exp(s3 - m[None])
        den = reduce_rows(e3, jnp.add) + jnp.exp(sink_row - m)
        pn = (e3 * (1.0 / den)[None]).reshape(BLOCK, stack)
        band = jnp.concatenate([jnp.where(tri, 0.0, pn), jnp.where(tri, pn, 0.0)], axis=0)
        return band.astype(_BF16)

    def sink_row(kvh, odd):
        cols = [jnp.full((SUBLANES, BLOCK), sink_ref[j, 2 * (PAIRS_PER_KV * kvh + i) + odd], _F32)
                for i in range(PAIRS_PER_KV)]
        return jnp.concatenate(cols, axis=1)

    def body(project, attend):
        if project:
            slot = _fetch_tile(step, in_copies)
            u_ref[...] = _rms(h_buf[slot], gpre_ref[layer:layer + 1, :]).astype(_BF16)
        if attend:
            kj = lax.broadcasted_iota(jnp.int32, (BLOCK, stack), 0)
            qi = lax.broadcasted_iota(jnp.int32, (BLOCK, stack), 1) & (BLOCK - 1)
            tri = kj <= qi
            prev_exists = (kj + ((step - 1) * TM - BLOCK)) >= 0

        for kvh in range(N_KV_HEADS):
            pairs = slice(PAIRS_PER_KV * kvh, PAIRS_PER_KV * (kvh + 1))
            if attend:
                scores = []
                for b in range(nb):
                    rows = slice(b * BLOCK, (b + 1) * BLOCK)
                    band = slice(b * BLOCK, (b + 2) * BLOCK)
                    qs = q3_ref[prv, pairs, rows, :].reshape(stack, LANES)
                    scores.append((_dot_nt(klo_ref[prv, kvh, band, :], qs),
                                   _dot_nt(khi_ref[prv, kvh, band, :], qs)))

            if project:
                q = _dot(u_ref[...], win_ref[:, kvh * per_kv:(kvh + 1) * per_kv])
                gate = _dot(u_ref[...],
                            win_ref[:, GATE_COL + kvh * per_kv:GATE_COL + (kvh + 1) * per_kv])
                for i in range(PAIRS_PER_KV):
                    p = PAIRS_PER_KV * kvh + i
                    q3_ref[cur, p] = (q[:, i * LANES:(i + 1) * LANES] * ATTN_SCALE).astype(_BF16)
                    g = gate[:, i * LANES:(i + 1) * LANES]
                    sg3_ref[cur, p] = g * jax.nn.sigmoid(g)

            if attend:
                sink_e = sink_row(kvh, 0)
                sink_o = sink_row(kvh, 1)
                for b in range(nb):
                    rows = slice(b * BLOCK, (b + 1) * BLOCK)
                    band = slice(b * BLOCK, (b + 2) * BLOCK)
                    vt = vt_ref[prv, kvh * HEAD_DIM:(kvh + 1) * HEAD_DIM, band]
                    o_e = _dot(vt, softmax_band(scores[b][0], sink_e, b, tri, prev_exists))
                    o_o = _dot(vt, softmax_band(scores[b][1], sink_o, b, tri, prev_exists))
                    o2 = jnp.concatenate([o_e, o_o], axis=0).T
                    sg = sg3_ref[prv, pairs, rows, :].reshape(stack, LANES)
                    y3_ref[pairs, rows, :] = (o2 * sg).astype(_BF16).reshape(PAIRS_PER_KV, BLOCK, LANES)

        if project:
            project_kv(first=not attend)

        if attend:
            y = jnp.concatenate([y3_ref[p] for p in range(N_PAIRS)], axis=1)
            yo = _dot(y, wout_ref[...])

            def store(oslot):
                o_buf[oslot] = h_buf[(step - 1) % H_SLOTS] + _rms(yo, gpost_ref[layer:layer + 1, :])
            _write_back(step, out_copies, store, last=not project)

    _staggered(step, body)


def _layer_spec(stacked, j):
    zeros = (0,) * (stacked.ndim - 1)
    return pl.BlockSpec((None,) + stacked.shape[1:], lambda i: (j,) + zeros,
                        pipeline_mode=pl.Buffered(1))


def _attn_layer(h, layer, j, norm_pre, norm_post, sinks, w_in, w_out, casts):
    band_rows = BLOCK + TM
    cast_in, cast_out, cast_shapes = _cast_specs(casts)
    return pl.pallas_call(
        functools.partial(_attn_kernel, layer, j, len(casts)),
        out_shape=[jax.ShapeDtypeStruct((SEQ, D_MODEL), _F32)] + cast_shapes,
        grid=(N_TILES + 1,),
        in_specs=[
            pl.BlockSpec(memory_space=pltpu.SMEM),
            pl.BlockSpec(memory_space=pl.ANY),
            _resident(norm_pre),
            _resident(norm_post),
            _resident(w_in),
            _resident(w_out),
        ] + cast_in,
        out_specs=[pl.BlockSpec(memory_space=pl.ANY)] + cast_out,
        scratch_shapes=[
            pltpu.VMEM((TM, D_MODEL), _BF16),
            pltpu.VMEM((2, N_PAIRS, TM, LANES), _BF16),
            pltpu.VMEM((2, N_PAIRS, TM, LANES), _F32),
            pltpu.VMEM((N_PAIRS, TM, LANES), _BF16),
            pltpu.VMEM((2, N_KV_HEADS, band_rows, LANES), _BF16),
            pltpu.VMEM((2, N_KV_HEADS, band_rows, LANES), _BF16),
            pltpu.VMEM((2, KV_W, band_rows), _BF16),
            pltpu.VMEM((H_SLOTS, TM, D_MODEL), _F32),
            pltpu.VMEM((O_SLOTS, TM, D_MODEL), _F32),
            pltpu.SemaphoreType.DMA((H_SLOTS,)),
            pltpu.SemaphoreType.DMA((O_SLOTS,)),
        ],
        compiler_params=pltpu.CompilerParams(
            dimension_semantics=("arbitrary",), vmem_limit_bytes=VMEM_LIMIT),
        name="attn_layer",
    )(sinks, h, norm_pre, norm_post, w_in, w_out, *[w for w, _ in casts])


SEG = TM // SUBLANES
CONV_TAIL = (CONV_W - 1) * SUBLANES


def _sublane_scan(a, b, row):
    for d in (1, 2, 4):
        keep = row >= d
        a_sh = jnp.where(keep, pltpu.roll(a, d, 0), 1.0)
        b_sh = jnp.where(keep, pltpu.roll(b, d, 0), 0.0)
        b = a * b_sh + b
        a = a * a_sh
    return a, b


def _lru_kernel(layer, j, n_cast, h_hbm, gpre_ref, gpost_ref, win_ref, cw_ref, cb_ref, wa_ref,
                ba_ref, wx_ref, bx_ref, lam_ref, wout_ref, *refs):
    cast_src, (out_hbm, *cast_dst), scratch = refs[:n_cast], refs[n_cast:2 * n_cast + 1], refs[2 * n_cast + 1:]
    (u_ref, x_ref, tail_ref, y_ref, yo_ref, sp_ref, hc_ref,
     h_buf, o_buf, in_sem, out_sem) = scratch
    _cast_rows(cast_src, cast_dst)
    step = pl.program_id(0)
    cur = step % 2
    prv = 1 - cur

    def in_copies(tile, slot):
        return [pltpu.make_async_copy(h_hbm.at[pl.ds(tile * TM + s * SEG, SEG), :],
                                      h_buf.at[slot, :, s, :], in_sem.at[slot])
                for s in range(SUBLANES)]

    def out_copies(tile, slot):
        return [pltpu.make_async_copy(o_buf.at[slot, :, s, :],
                                      out_hbm.at[pl.ds(tile * TM + s * SEG, SEG), :], out_sem.at[slot])
                for s in range(SUBLANES)]

    @pl.when(step == 0)
    def _():
        for c in in_copies(0, 0):
            c.start()
        tail_ref[...] = jnp.zeros(tail_ref.shape, _F32)
        hc_ref[...] = jnp.zeros(hc_ref.shape, _F32)
        nl = -lam_ref[j:j + 1, :]
        softplus = jnp.maximum(nl, 0.0) + jnp.log1p(jnp.exp(-jnp.abs(nl)))
        sp_ref[...] = jnp.broadcast_to(C_RG * softplus, sp_ref.shape)

    def block_cols(blk):
        return slice(blk * LRU_BLOCK_W, (blk + 1) * LRU_BLOCK_W)

    def project_x(blk):
        cs = block_cols(blk)
        x_ref[:, cs] = _dot(u_ref[...], win_ref[:, cs])

    def out_project(blk):
        cs = block_cols(blk)
        yo_ref[:, cs] = _dot(y_ref[prv], wout_ref[:, cs])

    def recur_block(blk, row, first_segment):
        cs = block_cols(blk)
        gate = _dot(u_ref[...],
                    win_ref[:, D_MODEL + blk * LRU_BLOCK_W:D_MODEL + (blk + 1) * LRU_BLOCK_W])

        x = x_ref[:, cs]
        tail_cur = x[TM - CONV_TAIL:]
        tail_prev = tail_ref[:, cs]
        tail_ref[:, cs] = tail_cur
        wrap = []
        for i in range(CONV_W - 1):
            rows = slice(i * SUBLANES, (i + 1) * SUBLANES)
            wrap.append(jnp.where(first_segment, pltpu.roll(tail_prev[rows], 1, 0),
                                  pltpu.roll(tail_cur[rows], 1, 0)))
        xc = cb_ref[j:j + 1, cs] + cw_ref[CONV_W - 1:CONV_W, cs] * x
        for shift in range(1, CONV_W):
            tap = CONV_W - 1 - shift
            shifted = jnp.concatenate(wrap[CONV_W - 1 - shift:] + [x[:TM - shift * SUBLANES]], axis=0)
            xc = xc + cw_ref[tap:tap + 1, cs] * shifted
        xcb = xc.astype(_BF16)
        r = jax.nn.sigmoid(_dot(xcb, wa_ref[blk]) + ba_ref[j, blk:blk + 1, :])
        ig = jax.nn.sigmoid(_dot(xcb, wx_ref[blk]) + bx_ref[j, blk:blk + 1, :])
        sg = gate * jax.nn.sigmoid(gate)

        nla = r * sp_ref[0:1, cs]
        a = jnp.exp(-nla)
        z = jnp.tanh(nla) * (a * a + 1.0)
        mult = jnp.where(z > 0.0, z * lax.rsqrt(z), 0.0)
        b = mult * (ig * xc)

        acc_a = a[0:SUBLANES]
        acc_h = b[0:SUBLANES]
        prods, local = [acc_a], [acc_h]
        for g in range(1, SEG):
            rows = slice(g * SUBLANES, (g + 1) * SUBLANES)
            acc_h = a[rows] * acc_h + b[rows]
            acc_a = a[rows] * acc_a
            prods.append(acc_a)
            local.append(acc_h)
        carry_in = hc_ref[:, cs]
        seg_a, seg_h = _sublane_scan(acc_a, acc_h, row)
        seg_end = seg_h + seg_a * carry_in
        entering = jnp.where(first_segment, carry_in, pltpu.roll(seg_end, 1, 0))
        hc_ref[:, cs] = jnp.broadcast_to(seg_end[SUBLANES - 1:SUBLANES, :], (SUBLANES, LRU_BLOCK_W))
        ys = [(local[g] + prods[g] * entering) * sg[g * SUBLANES:(g + 1) * SUBLANES]
              for g in range(SEG)]
        y_ref[cur, :, cs] = jnp.concatenate(ys, axis=0).astype(_BF16)

    def body(recur, finish):
        if recur:
            slot = _fetch_tile(step, in_copies)
        if finish:
            out_project(0)
            out_project(1)
        if recur:
            row = lax.broadcasted_iota(jnp.int32, (SUBLANES, LRU_BLOCK_W), 0)
            first_segment = row == 0
            u_ref[...] = _rms(h_buf[slot].reshape(TM, D_MODEL),
                              gpre_ref[layer:layer + 1, :]).astype(_BF16)
            project_x(0)
        for blk in range(N_LRU_BLOCKS):
            if recur and blk + 1 < N_LRU_BLOCKS:
                project_x(blk + 1)
            if finish and blk + 2 < N_LRU_BLOCKS:
                out_project(blk + 2)
            if recur:
                recur_block(blk, row, first_segment)
        if finish:
            def store(oslot):
                h_prev = h_buf[(step - 1) % H_SLOTS].reshape(TM, D_MODEL)
                out = h_prev + _rms(yo_ref[...], gpost_ref[layer:layer + 1, :])
                o_buf[oslot] = out.reshape(SEG, SUBLANES, D_MODEL)
            _write_back(step, out_copies, store, last=not recur)

    _staggered(step, body)


def _lru_layer(h, layer, j, norm_pre, norm_post, small, weights, casts):
    conv_w, conv_b, b_a, b_x, lam = small
    w_in, w_a, w_x, w_out = weights
    gate_shape = (N_LRU_BLOCKS, LRU_BLOCK_W, LRU_BLOCK_W)
    w_a, w_x = w_a.reshape(gate_shape), w_x.reshape(gate_shape)
    cast_in, cast_out, cast_shapes = _cast_specs(casts)
    return pl.pallas_call(
        functools.partial(_lru_kernel, layer, j, len(casts)),
        out_shape=[jax.ShapeDtypeStruct((SEQ, D_MODEL), _F32)] + cast_shapes,
        grid=(N_TILES + 1,),
        in_specs=[
            pl.BlockSpec(memory_space=pl.ANY),
            _resident(norm_pre),
            _resident(norm_post),
            _resident(w_in), _layer_spec(conv_w, j), _resident(conv_b),
            _resident(w_a), _resident(b_a), _resident(w_x), _resident(b_x),
            _resident(lam), _resident(w_out),
        ] + cast_in,
        out_specs=[pl.BlockSpec(memory_space=pl.ANY)] + cast_out,
        scratch_shapes=[
            pltpu.VMEM((TM, D_MODEL), _BF16),
            pltpu.VMEM((TM, D_MODEL), _F32),
            pltpu.VMEM((CONV_TAIL, D_MODEL), _F32),
            pltpu.VMEM((2, TM, D_MODEL), _BF16),
            pltpu.VMEM((TM, D_MODEL), _F32),
            pltpu.VMEM((SUBLANES, D_MODEL), _F32),
            pltpu.VMEM((SUBLANES, D_MODEL), _F32),
            pltpu.VMEM((H_SLOTS, SEG, SUBLANES, D_MODEL), _F32),
            pltpu.VMEM((O_SLOTS, SEG, SUBLANES, D_MODEL), _F32),
            pltpu.SemaphoreType.DMA((H_SLOTS,)),
            pltpu.SemaphoreType.DMA((O_SLOTS,)),
        ],
        compiler_params=pltpu.CompilerParams(
            dimension_semantics=("arbitrary",), vmem_limit_bytes=VMEM_LIMIT),
        name="lru_layer",
    )(h, norm_pre, norm_post, w_in, conv_w, conv_b, w_a, b_a, w_x, b_x, lam, w_out,
      *[w for w, _ in casts])


def kernel(x, norm_pre, norm_post, attn_w_in, attn_w_out, attn_sinks, lru_w_in, lru_conv_w,
           lru_conv_b, lru_w_a, lru_b_a, lru_w_x, lru_b_x, lru_lambda, lru_w_out):
    depth = norm_pre.shape[0]
    lru_small = (lru_conv_w, lru_conv_b, lru_b_a, lru_b_x, lru_lambda)
    rows = lambda w: w.reshape(w.shape[0], D_MODEL, -1)
    attn_f32 = (attn_w_in, attn_w_out)
    lru_f32 = (lru_w_in, rows(lru_w_a), rows(lru_w_x), lru_w_out)
    weights = [w[0].astype(_BF16) for w in attn_f32]
    h = x.reshape(SEQ, D_MODEL)
    for layer in range(depth):
        j = layer // 2
        last = layer == depth - 1
        if layer % 2 == 0:
            casts = [] if last else [(w, j) for w in lru_f32]
            h, *weights = _attn_layer(h, layer, j, norm_pre, norm_post, attn_sinks, *weights, casts)
        else:
            casts = [] if last else [(w, j + 1) for w in attn_f32]
            h, *weights = _lru_layer(h, layer, j, norm_pre, norm_post, lru_small, weights, casts)
    return h.reshape(x.shape)
```

```python
import functools

import jax
import jax.numpy as jnp
from jax import lax
from jax.experimental import pallas as pl
from jax.experimental.pallas import tpu as pltpu

D_MODEL = 2048
SEQ = 8192
HEAD_DIM = 64
N_HEADS = 32
N_KV_HEADS = 4
GROUP = N_HEADS // N_KV_HEADS
BLOCK = 128
KV_W = N_KV_HEADS * HEAD_DIM
GATE_COL = D_MODEL + 2 * KV_W
LRU_BLOCK_W = 256
N_LRU_BLOCKS = D_MODEL // LRU_BLOCK_W
CONV_W = 4
C_RG = 8.0
NORM_EPS = 1e-6
MASK_VALUE = -1e30
ATTN_SCALE = HEAD_DIM ** -0.5

LANES = 128
SUBLANES = 8
N_PAIRS = N_HEADS // 2
PAIRS_PER_KV = GROUP // 2

TM = 256
N_TILES = SEQ // TM
VMEM_LIMIT = 60 * 1024 * 1024

_F32 = jnp.float32
_BF16 = jnp.bfloat16


def _rms(x, g):
    ms = jnp.mean(x * x, axis=-1, keepdims=True)
    return x * lax.rsqrt(ms + NORM_EPS) * g


def _dot(a, b):
    return jnp.dot(a, b, preferred_element_type=_F32)


def _dot_nt(a, b):
    return lax.dot_general(a, b, (((1,), (1,)), ((), ())), preferred_element_type=_F32)


CAST_ROWS = D_MODEL // N_TILES


def _cast_rows(srcs, dsts):
    for src, dst in zip(srcs, dsts):
        dst[...] = src[...].astype(_BF16)


def _cast_specs(casts):
    rows = lambda i: jnp.minimum(i, N_TILES - 1)
    in_specs = [pl.BlockSpec((None, CAST_ROWS, w.shape[-1]), lambda i, j=j: (j, rows(i), 0))
                for w, j in casts]
    out_specs = [pl.BlockSpec((CAST_ROWS, w.shape[-1]), lambda i: (rows(i), 0)) for w, _ in casts]
    out_shapes = [jax.ShapeDtypeStruct((D_MODEL, w.shape[-1]), _BF16) for w, _ in casts]
    return in_specs, out_specs, out_shapes


def _resident(arr):
    zeros = (0,) * arr.ndim
    return pl.BlockSpec(arr.shape, lambda i: zeros, pipeline_mode=pl.Buffered(1))


H_SLOTS = 3
O_SLOTS = 2


def _fetch_tile(step, in_copies):
    slot = step % H_SLOTS
    for c in in_copies(step, slot):
        c.wait()

    @pl.when(step + 1 < N_TILES)
    def _():
        for c in in_copies(step + 1, (step + 1) % H_SLOTS):
            c.start()
    return slot


def _write_back(step, out_copies, store, last):
    slot = (step - 1) % O_SLOTS

    @pl.when(step >= 1 + O_SLOTS)
    def _():
        for c in out_copies(step - 1 - O_SLOTS, slot):
            c.wait()
    store(slot)
    for c in out_copies(step - 1, slot):
        c.start()
    if last:
        for c in out_copies(step - 2, 1 - slot) + out_copies(step - 1, slot):
            c.wait()


class _WeightLoader:
    def __init__(self, sem):
        self._sem = sem
        self._copies = {}

    def add(self, key, src, dst):
        self._copies[key] = pltpu.make_async_copy(src, dst, self._sem.at[len(self._copies)])

    def start_all(self):
        for c in self._copies.values():
            c.start()

    def wait(self, key):
        self._copies[key].wait()


def _staggered(step, body):
    pl.when(step == 0)(lambda: body(True, False))
    pl.when(jnp.logical_and(step > 0, step < N_TILES))(lambda: body(True, True))
    pl.when(step == N_TILES)(lambda: body(False, True))


def _attn_kernel(layer, j, n_cast, sink_ref, h_hbm, gpre_ref, gpost_ref, win_hbm, wout_hbm, *refs):
    cast_src, (out_hbm, *cast_dst), scratch = refs[:n_cast], refs[n_cast:2 * n_cast + 1], refs[2 * n_cast + 1:]
    (u_ref, q3_ref, sg3_ref, y3_ref, klo_ref, khi_ref, vt_ref,
     h_buf, o_buf, in_sem, out_sem, win_ref, wout_ref, w_sem) = scratch
    _cast_rows(cast_src, cast_dst)
    step = pl.program_id(0)
    cur = step % 2
    prv = 1 - cur
    per_kv = GROUP * HEAD_DIM

    weights = _WeightLoader(w_sem)
    for kvh in range(N_KV_HEADS):
        for key, col in (("q", kvh * per_kv), ("gate", GATE_COL + kvh * per_kv)):
            cols = slice(col, col + per_kv)
            weights.add((key, kvh), win_hbm.at[:, cols], win_ref.at[:, cols])
    weights.add("kv", win_hbm.at[:, D_MODEL:GATE_COL], win_ref.at[:, D_MODEL:GATE_COL])
    weights.add("out", wout_hbm, wout_ref)

    def in_copies(tile, slot):
        return [pltpu.make_async_copy(h_hbm.at[pl.ds(tile * TM, TM), :], h_buf.at[slot],
                                      in_sem.at[slot])]

    def out_copies(tile, slot):
        return [pltpu.make_async_copy(o_buf.at[slot], out_hbm.at[pl.ds(tile * TM, TM), :],
                                      out_sem.at[slot])]

    @pl.when(step == 0)
    def _():
        for c in in_copies(0, 0):
            c.start()
        weights.start_all()

    nb = TM // BLOCK
    stack = PAIRS_PER_KV * BLOCK
    chunks = BLOCK // SUBLANES

    def project_kv(first):
        if first:
            weights.wait("kv")
        kv = _dot(u_ref[...], win_ref[:, D_MODEL:GATE_COL])
        lane_lo = lax.broadcasted_iota(jnp.int32, (TM, LANES), 1) < HEAD_DIM
        new = slice(BLOCK, BLOCK + TM)
        for c in range(N_KV_HEADS // 2):
            orig = kv[:, c * LANES:(c + 1) * LANES]
            sw = pltpu.roll(orig, HEAD_DIM, 1)
            klo_ref[cur, 2 * c, new, :] = jnp.where(lane_lo, orig, 0.0).astype(_BF16)
            khi_ref[cur, 2 * c, new, :] = jnp.where(lane_lo, 0.0, sw).astype(_BF16)
            klo_ref[cur, 2 * c + 1, new, :] = jnp.where(lane_lo, sw, 0.0).astype(_BF16)
            khi_ref[cur, 2 * c + 1, new, :] = jnp.where(lane_lo, 0.0, orig).astype(_BF16)
        vt_ref[cur, :, new] = kv[:, KV_W:2 * KV_W].T.astype(_BF16)
        for ref in (klo_ref, khi_ref):
            prev = jnp.zeros((N_KV_HEADS, BLOCK, LANES), _BF16) if first else ref[prv, :, TM:TM + BLOCK, :]
            ref[cur, :, 0:BLOCK, :] = prev
        vt_ref[cur, :, 0:BLOCK] = (jnp.zeros((KV_W, BLOCK), _BF16) if first
                                   else vt_ref[prv, :, TM:TM + BLOCK])

    def reduce_rows(x3, op):
        r = x3[0]
        for c in range(1, chunks):
            r = op(r, x3[c])
        for d in (1, 2, 4):
            r = op(r, pltpu.roll(r, d, 0))
        return r

    def softmax_band(s_full, sink_row, b, tri, prev_exists):
        s_prev = s_full[0:BLOCK]
        s_cur = s_full[BLOCK:2 * BLOCK]
        if b == 0:
            s_prev = jnp.where(prev_exists, s_prev, MASK_VALUE)
        s3 = jnp.where(tri, s_cur, s_prev).reshape(chunks, SUBLANES, stack)
        m = jnp.maximum(reduce_rows(s3, jnp.maximum), sink_row)
        e3 = jnp.exp(s3 - m[None])
        den = reduce_rows(e3, jnp.add) + jnp.exp(sink_row - m)
        pn = (e3 * (1.0 / den)[None]).reshape(BLOCK, stack)
        band = jnp.concatenate([jnp.where(tri, 0.0, pn), jnp.where(tri, pn, 0.0)], axis=0)
        return band.astype(_BF16)

    def sink_row(kvh, odd):
        cols = [jnp.full((SUBLANES, BLOCK), sink_ref[j, 2 * (PAIRS_PER_KV * kvh + i) + odd], _F32)
                for i in range(PAIRS_PER_KV)]
        return jnp.concatenate(cols, axis=1)

    def body(project, attend):
        if project:
            slot = _fetch_tile(step, in_copies)
            u_ref[...] = _rms(h_buf[slot], gpre_ref[layer:layer + 1, :]).astype(_BF16)
        if attend:
            kj = lax.broadcasted_iota(jnp.int32, (BLOCK, stack), 0)
            qi = lax.broadcasted_iota(jnp.int32, (BLOCK, stack), 1) & (BLOCK - 1)
            tri = kj <= qi
            prev_exists = (kj + ((step - 1) * TM - BLOCK)) >= 0

        for kvh in range(N_KV_HEADS):
            pairs = slice(PAIRS_PER_KV * kvh, PAIRS_PER_KV * (kvh + 1))
            if attend:
                scores = []
                for b in range(nb):
                    rows = slice(b * BLOCK, (b + 1) * BLOCK)
                    band = slice(b * BLOCK, (b + 2) * BLOCK)
                    qs = q3_ref[prv, pairs, rows, :].reshape(stack, LANES)
                    scores.append((_dot_nt(klo_ref[prv, kvh, band, :], qs),
                                   _dot_nt(khi_ref[prv, kvh, band, :], qs)))

            if project:
                if not attend:
                    weights.wait(("q", kvh))
                    weights.wait(("gate", kvh))
                q = _dot(u_ref[...], win_ref[:, kvh * per_kv:(kvh + 1) * per_kv])
                gate = _dot(u_ref[...],
                            win_ref[:, GATE_COL + kvh * per_kv:GATE_COL + (kvh + 1) * per_kv])
                for i in range(PAIRS_PER_KV):
                    p = PAIRS_PER_KV * kvh + i
                    q3_ref[cur, p] = (q[:, i * LANES:(i + 1) * LANES] * ATTN_SCALE).astype(_BF16)
                    g = gate[:, i * LANES:(i + 1) * LANES]
                    sg3_ref[cur, p] = g * jax.nn.sigmoid(g)

            if attend:
                sink_e = sink_row(kvh, 0)
                sink_o = sink_row(kvh, 1)
                for b in range(nb):
                    rows = slice(b * BLOCK, (b + 1) * BLOCK)
                    band = slice(b * BLOCK, (b + 2) * BLOCK)
                    vt = vt_ref[prv, kvh * HEAD_DIM:(kvh + 1) * HEAD_DIM, band]
                    o_e = _dot(vt, softmax_band(scores[b][0], sink_e, b, tri, prev_exists))
                    o_o = _dot(vt, softmax_band(scores[b][1], sink_o, b, tri, prev_exists))
                    o2 = jnp.concatenate([o_e, o_o], axis=0).T
                    sg = sg3_ref[prv, pairs, rows, :].reshape(stack, LANES)
                    y3_ref[pairs, rows, :] = (o2 * sg).astype(_BF16).reshape(PAIRS_PER_KV, BLOCK, LANES)

        if project:
            project_kv(first=not attend)

        if attend:
            y = jnp.concatenate([y3_ref[p] for p in range(N_PAIRS)], axis=1)
            yo = _dot(y, wout_ref[...])

            def store(oslot):
                o_buf[oslot] = h_buf[(step - 1) % H_SLOTS] + _rms(yo, gpost_ref[layer:layer + 1, :])
            _write_back(step, out_copies, store, last=not project)
        else:
            weights.wait("out")

    _staggered(step, body)


def _layer_spec(stacked, j):
    zeros = (0,) * (stacked.ndim - 1)
    return pl.BlockSpec((None,) + stacked.shape[1:], lambda i: (j,) + zeros,
                        pipeline_mode=pl.Buffered(1))


def _attn_layer(h, layer, j, norm_pre, norm_post, sinks, w_in, w_out, casts):
    band_rows = BLOCK + TM
    cast_in, cast_out, cast_shapes = _cast_specs(casts)
    return pl.pallas_call(
        functools.partial(_attn_kernel, layer, j, len(casts)),
        out_shape=[jax.ShapeDtypeStruct((SEQ, D_MODEL), _F32)] + cast_shapes,
        grid=(N_TILES + 1,),
        in_specs=[
            pl.BlockSpec(memory_space=pltpu.SMEM),
            pl.BlockSpec(memory_space=pl.ANY),
            _resident(norm_pre),
            _resident(norm_post),
            pl.BlockSpec(memory_space=pl.ANY),
            pl.BlockSpec(memory_space=pl.ANY),
        ] + cast_in,
        out_specs=[pl.BlockSpec(memory_space=pl.ANY)] + cast_out,
        scratch_shapes=[
            pltpu.VMEM((TM, D_MODEL), _BF16),
            pltpu.VMEM((2, N_PAIRS, TM, LANES), _BF16),
            pltpu.VMEM((2, N_PAIRS, TM, LANES), _F32),
            pltpu.VMEM((N_PAIRS, TM, LANES), _BF16),
            pltpu.VMEM((2, N_KV_HEADS, band_rows, LANES), _BF16),
            pltpu.VMEM((2, N_KV_HEADS, band_rows, LANES), _BF16),
            pltpu.VMEM((2, KV_W, band_rows), _BF16),
            pltpu.VMEM((H_SLOTS, TM, D_MODEL), _F32),
            pltpu.VMEM((O_SLOTS, TM, D_MODEL), _F32),
            pltpu.SemaphoreType.DMA((H_SLOTS,)),
            pltpu.SemaphoreType.DMA((O_SLOTS,)),
            pltpu.VMEM(w_in.shape, _BF16),
            pltpu.VMEM(w_out.shape, _BF16),
            pltpu.SemaphoreType.DMA((2 * N_KV_HEADS + 2,)),
        ],
        compiler_params=pltpu.CompilerParams(
            dimension_semantics=("arbitrary",), vmem_limit_bytes=VMEM_LIMIT),
        name="attn_layer",
    )(sinks, h, norm_pre, norm_post, w_in, w_out, *[w for w, _ in casts])


SEG = TM // SUBLANES
CONV_TAIL = (CONV_W - 1) * SUBLANES


def _sublane_scan(a, b, row):
    for d in (1, 2, 4):
        keep = row >= d
        a_sh = jnp.where(keep, pltpu.roll(a, d, 0), 1.0)
        b_sh = jnp.where(keep, pltpu.roll(b, d, 0), 0.0)
        b = a * b_sh + b
        a = a * a_sh
    return a, b


def _lru_kernel(layer, j, n_cast, h_hbm, gpre_ref, gpost_ref, win_hbm, cw_ref, cb_ref, wa_hbm,
                ba_ref, wx_hbm, bx_ref, lam_ref, wout_hbm, *refs):
    cast_src, (out_hbm, *cast_dst), scratch = refs[:n_cast], refs[n_cast:2 * n_cast + 1], refs[2 * n_cast + 1:]
    (u_ref, x_ref, tail_ref, y_ref, yo_ref, sp_ref, hc_ref,
     h_buf, o_buf, in_sem, out_sem, win_ref, wa_ref, wx_ref, wout_ref, w_sem) = scratch
    _cast_rows(cast_src, cast_dst)
    step = pl.program_id(0)
    cur = step % 2
    prv = 1 - cur

    weights = _WeightLoader(w_sem)
    for blk in range(N_LRU_BLOCKS):
        for key, col in (("x", blk * LRU_BLOCK_W), ("gate", D_MODEL + blk * LRU_BLOCK_W)):
            cols = slice(col, col + LRU_BLOCK_W)
            weights.add((key, blk), win_hbm.at[:, cols], win_ref.at[:, cols])
        if blk == 0:
            weights.add("a", wa_hbm, wa_ref)
            weights.add("i", wx_hbm, wx_ref)
    weights.add("out", wout_hbm, wout_ref)
    first_step = [False]

    def in_copies(tile, slot):
        return [pltpu.make_async_copy(h_hbm.at[pl.ds(tile * TM + s * SEG, SEG), :],
                                      h_buf.at[slot, :, s, :], in_sem.at[slot])
                for s in range(SUBLANES)]

    def out_copies(tile, slot):
        return [pltpu.make_async_copy(o_buf.at[slot, :, s, :],
                                      out_hbm.at[pl.ds(tile * TM + s * SEG, SEG), :], out_sem.at[slot])
                for s in range(SUBLANES)]

    @pl.when(step == 0)
    def _():
        for c in in_copies(0, 0):
            c.start()
        weights.start_all()
        tail_ref[...] = jnp.zeros(tail_ref.shape, _F32)
        hc_ref[...] = jnp.zeros(hc_ref.shape, _F32)
        nl = -lam_ref[j:j + 1, :]
        softplus = jnp.maximum(nl, 0.0) + jnp.log1p(jnp.exp(-jnp.abs(nl)))
        sp_ref[...] = jnp.broadcast_to(C_RG * softplus, sp_ref.shape)

    def block_cols(blk):
        return slice(blk * LRU_BLOCK_W, (blk + 1) * LRU_BLOCK_W)

    def project_x(blk):
        cs = block_cols(blk)
        if first_step[0]:
            weights.wait(("x", blk))
        x_ref[:, cs] = _dot(u_ref[...], win_ref[:, cs])

    def out_project(blk):
        cs = block_cols(blk)
        yo_ref[:, cs] = _dot(y_ref[prv], wout_ref[:, cs])

    def recur_block(blk, row, first_segment):
        cs = block_cols(blk)
        if first_step[0]:
            weights.wait(("gate", blk))
            if blk == 0:
                weights.wait("a")
                weights.wait("i")
        gate = _dot(u_ref[...],
                    win_ref[:, D_MODEL + blk * LRU_BLOCK_W:D_MODEL + (blk + 1) * LRU_BLOCK_W])

        x = x_ref[:, cs]
        tail_cur = x[TM - CONV_TAIL:]
        tail_prev = tail_ref[:, cs]
        tail_ref[:, cs] = tail_cur
        wrap = []
        for i in range(CONV_W - 1):
            rows = slice(i * SUBLANES, (i + 1) * SUBLANES)
            wrap.append(jnp.where(first_segment, pltpu.roll(tail_prev[rows], 1, 0),
                                  pltpu.roll(tail_cur[rows], 1, 0)))
        xc = cb_ref[j:j + 1, cs] + cw_ref[CONV_W - 1:CONV_W, cs] * x
        for shift in range(1, CONV_W):
            tap = CONV_W - 1 - shift
            shifted = jnp.concatenate(wrap[CONV_W - 1 - shift:] + [x[:TM - shift * SUBLANES]], axis=0)
            xc = xc + cw_ref[tap:tap + 1, cs] * shifted
        xcb = xc.astype(_BF16)
        r = jax.nn.sigmoid(_dot(xcb, wa_ref[blk]) + ba_ref[j, blk:blk + 1, :])
        ig = jax.nn.sigmoid(_dot(xcb, wx_ref[blk]) + bx_ref[j, blk:blk + 1, :])
        sg = gate * jax.nn.sigmoid(gate)

        nla = r * sp_ref[0:1, cs]
        a = jnp.exp(-nla)
        z = jnp.tanh(nla) * (a * a + 1.0)
        mult = jnp.where(z > 0.0, z * lax.rsqrt(z), 0.0)
        b = mult * (ig * xc)

        acc_a = a[0:SUBLANES]
        acc_h = b[0:SUBLANES]
        prods, local = [acc_a], [acc_h]
        for g in range(1, SEG):
            rows = slice(g * SUBLANES, (g + 1) * SUBLANES)
            acc_h = a[rows] * acc_h + b[rows]
            acc_a = a[rows] * acc_a
            prods.append(acc_a)
            local.append(acc_h)
        carry_in = hc_ref[:, cs]
        seg_a, seg_h = _sublane_scan(acc_a, acc_h, row)
        seg_end = seg_h + seg_a * carry_in
        entering = jnp.where(first_segment, carry_in, pltpu.roll(seg_end, 1, 0))
        hc_ref[:, cs] = jnp.broadcast_to(seg_end[SUBLANES - 1:SUBLANES, :], (SUBLANES, LRU_BLOCK_W))
        ys = [(local[g] + prods[g] * entering) * sg[g * SUBLANES:(g + 1) * SUBLANES]
              for g in range(SEG)]
        y_ref[cur, :, cs] = jnp.concatenate(ys, axis=0).astype(_BF16)

    def body(recur, finish):
        first_step[0] = not finish
        if recur:
            slot = _fetch_tile(step, in_copies)
        if finish:
            out_project(0)
            out_project(1)
        if recur:
            row = lax.broadcasted_iota(jnp.int32, (SUBLANES, LRU_BLOCK_W), 0)
            first_segment = row == 0
            u_ref[...] = _rms(h_buf[slot].reshape(TM, D_MODEL),
                              gpre_ref[layer:layer + 1, :]).astype(_BF16)
            project_x(0)
        for blk in range(N_LRU_BLOCKS):
            if recur and blk + 1 < N_LRU_BLOCKS:
                project_x(blk + 1)
            if finish and blk + 2 < N_LRU_BLOCKS:
                out_project(blk + 2)
            if recur:
                recur_block(blk, row, first_segment)
        if finish:
            def store(oslot):
                h_prev = h_buf[(step - 1) % H_SLOTS].reshape(TM, D_MODEL)
                out = h_prev + _rms(yo_ref[...], gpost_ref[layer:layer + 1, :])
                o_buf[oslot] = out.reshape(SEG, SUBLANES, D_MODEL)
            _write_back(step, out_copies, store, last=not recur)
        else:
            weights.wait("out")

    _staggered(step, body)


def _lru_layer(h, layer, j, norm_pre, norm_post, small, weights, casts):
    conv_w, conv_b, b_a, b_x, lam = small
    w_in, w_a, w_x, w_out = weights
    gate_shape = (N_LRU_BLOCKS, LRU_BLOCK_W, LRU_BLOCK_W)
    w_a, w_x = w_a.reshape(gate_shape), w_x.reshape(gate_shape)
    cast_in, cast_out, cast_shapes = _cast_specs(casts)
    in_hbm = pl.BlockSpec(memory_space=pl.ANY)
    return pl.pallas_call(
        functools.partial(_lru_kernel, layer, j, len(casts)),
        out_shape=[jax.ShapeDtypeStruct((SEQ, D_MODEL), _F32)] + cast_shapes,
        grid=(N_TILES + 1,),
        in_specs=[
            pl.BlockSpec(memory_space=pl.ANY),
            _resident(norm_pre),
            _resident(norm_post),
            in_hbm, _layer_spec(conv_w, j), _resident(conv_b),
            in_hbm, _resident(b_a), in_hbm, _resident(b_x),
            _resident(lam), in_hbm,
        ] + cast_in,
        out_specs=[pl.BlockSpec(memory_space=pl.ANY)] + cast_out,
        scratch_shapes=[
            pltpu.VMEM((TM, D_MODEL), _BF16),
            pltpu.VMEM((TM, D_MODEL), _F32),
            pltpu.VMEM((CONV_TAIL, D_MODEL), _F32),
            pltpu.VMEM((2, TM, D_MODEL), _BF16),
            pltpu.VMEM((TM, D_MODEL), _F32),
            pltpu.VMEM((SUBLANES, D_MODEL), _F32),
            pltpu.VMEM((SUBLANES, D_MODEL), _F32),
            pltpu.VMEM((H_SLOTS, SEG, SUBLANES, D_MODEL), _F32),
            pltpu.VMEM((O_SLOTS, SEG, SUBLANES, D_MODEL), _F32),
            pltpu.SemaphoreType.DMA((H_SLOTS,)),
            pltpu.SemaphoreType.DMA((O_SLOTS,)),
            pltpu.VMEM(w_in.shape, _BF16), pltpu.VMEM(w_a.shape, _BF16),
            pltpu.VMEM(w_x.shape, _BF16), pltpu.VMEM(w_out.shape, _BF16),
            pltpu.SemaphoreType.DMA((2 * N_LRU_BLOCKS + 3,)),
        ],
        compiler_params=pltpu.CompilerParams(
            dimension_semantics=("arbitrary",), vmem_limit_bytes=VMEM_LIMIT),
        name="lru_layer",
    )(h, norm_pre, norm_post, w_in, conv_w, conv_b, w_a, b_a, w_x, b_x, lam, w_out,
      *[w for w, _ in casts])


def kernel(x, norm_pre, norm_post, attn_w_in, attn_w_out, attn_sinks, lru_w_in, lru_conv_w,
           lru_conv_b, lru_w_a, lru_b_a, lru_w_x, lru_b_x, lru_lambda, lru_w_out):
    depth = norm_pre.shape[0]
    lru_small = (lru_conv_w, lru_conv_b, lru_b_a, lru_b_x, lru_lambda)
    rows = lambda w: w.reshape(w.shape[0], D_MODEL, -1)
    attn_f32 = (attn_w_in, attn_w_out)
    lru_f32 = (lru_w_in, rows(lru_w_a), rows(lru_w_x), lru_w_out)
    weights = [w[0].astype(_BF16) for w in attn_f32]
    h = x.reshape(SEQ, D_MODEL)
    for layer in range(depth):
        j = layer // 2
        last = layer == depth - 1
        if layer % 2 == 0:
            casts = [] if last else [(w, j) for w in lru_f32]
            h, *weights = _attn_layer(h, layer, j, norm_pre, norm_post, attn_sinks, *weights, casts)
        else:
            casts = [] if last else [(w, j + 1) for w in attn_f32]
            h, *weights = _lru_layer(h, layer, j, norm_pre, norm_post, lru_small, weights, casts)
    return h.reshape(x.shape)
```

```python
import functools

import jax
import jax.numpy as jnp
from jax import lax
from jax.experimental import pallas as pl
from jax.experimental.pallas import tpu as pltpu

D_MODEL = 2048
SEQ = 8192
HEAD_DIM = 64
N_HEADS = 32
N_KV_HEADS = 4
GROUP = N_HEADS // N_KV_HEADS
BLOCK = 128
KV_W = N_KV_HEADS * HEAD_DIM
GATE_COL = D_MODEL + 2 * KV_W
LRU_BLOCK_W = 256
N_LRU_BLOCKS = D_MODEL // LRU_BLOCK_W
CONV_W = 4
C_RG = 8.0
NORM_EPS = 1e-6
MASK_VALUE = -1e30
ATTN_SCALE = HEAD_DIM ** -0.5

LANES = 128
SUBLANES = 8
N_PAIRS = N_HEADS // 2
PAIRS_PER_KV = GROUP // 2

TM = 256
N_TILES = SEQ // TM
VMEM_LIMIT = 60 * 1024 * 1024

_F32 = jnp.float32
_BF16 = jnp.bfloat16


def _rms(x, g):
    ms = jnp.mean(x * x, axis=-1, keepdims=True)
    return x * lax.rsqrt(ms + NORM_EPS) * g


def _dot(a, b):
    return jnp.dot(a, b, preferred_element_type=_F32)


def _dot_nt(a, b):
    return lax.dot_general(a, b, (((1,), (1,)), ((), ())), preferred_element_type=_F32)


CAST_ROWS = D_MODEL // N_TILES


def _cast_rows(srcs, dsts):
    for src, dst in zip(srcs, dsts):
        dst[...] = src[...].astype(_BF16)


def _cast_specs(casts):
    rows = lambda i: jnp.minimum(i, N_TILES - 1)
    in_specs = [pl.BlockSpec((None, CAST_ROWS, w.shape[-1]), lambda i, j=j: (j, rows(i), 0))
                for w, j in casts]
    out_specs = [pl.BlockSpec((CAST_ROWS, w.shape[-1]), lambda i: (rows(i), 0)) for w, _ in casts]
    out_shapes = [jax.ShapeDtypeStruct((D_MODEL, w.shape[-1]), _BF16) for w, _ in casts]
    return in_specs, out_specs, out_shapes


def _resident(arr):
    zeros = (0,) * arr.ndim
    return pl.BlockSpec(arr.shape, lambda i: zeros, pipeline_mode=pl.Buffered(1))


H_SLOTS = 3
O_SLOTS = 2
TILE_DMA_QUEUE = 1


def _fetch_tile(step, in_copies):
    slot = step % H_SLOTS
    for c in in_copies(step, slot):
        c.wait()

    @pl.when(step + 1 < N_TILES)
    def _():
        for c in in_copies(step + 1, (step + 1) % H_SLOTS):
            c.start(priority=TILE_DMA_QUEUE)
    return slot


def _write_back(step, out_copies, store, last):
    slot = (step - 1) % O_SLOTS

    @pl.when(step >= 1 + O_SLOTS)
    def _():
        for c in out_copies(step - 1 - O_SLOTS, slot):
            c.wait()
    store(slot)
    for c in out_copies(step - 1, slot):
        c.start(priority=TILE_DMA_QUEUE)
    if last:
        for c in out_copies(step - 2, 1 - slot) + out_copies(step - 1, slot):
            c.wait()


def _staggered(step, body):
    pl.when(step == 0)(lambda: body(True, False))
    pl.when(jnp.logical_and(step > 0, step < N_TILES))(lambda: body(True, True))
    pl.when(step == N_TILES)(lambda: body(False, True))


def _attn_kernel(layer, j, n_cast, sink_ref, h_hbm, gpre_ref, gpost_ref, win_ref, wout_ref, *refs):
    cast_src, (out_hbm, *cast_dst), scratch = refs[:n_cast], refs[n_cast:2 * n_cast + 1], refs[2 * n_cast + 1:]
    (u_ref, q3_ref, sg3_ref, y3_ref, klo_ref, khi_ref, vt_ref,
     h_buf, o_buf, in_sem, out_sem) = scratch
    _cast_rows(cast_src, cast_dst)
    step = pl.program_id(0)
    cur = step % 2
    prv = 1 - cur

    def in_copies(tile, slot):
        return [pltpu.make_async_copy(h_hbm.at[pl.ds(tile * TM, TM), :], h_buf.at[slot],
                                      in_sem.at[slot])]

    def out_copies(tile, slot):
        return [pltpu.make_async_copy(o_buf.at[slot], out_hbm.at[pl.ds(tile * TM, TM), :],
                                      out_sem.at[slot])]

    @pl.when(step == 0)
    def _():
        for c in in_copies(0, 0):
            c.start(priority=TILE_DMA_QUEUE)

    nb = TM // BLOCK
    stack = PAIRS_PER_KV * BLOCK
    chunks = BLOCK // SUBLANES
    per_kv = GROUP * HEAD_DIM

    def project_kv(first):
        kv = _dot(u_ref[...], win_ref[:, D_MODEL:GATE_COL])
        lane_lo = lax.broadcasted_iota(jnp.int32, (TM, LANES), 1) < HEAD_DIM
        new = slice(BLOCK, BLOCK + TM)
        for c in range(N_KV_HEADS // 2):
            orig = kv[:, c * LANES:(c + 1) * LANES]
            sw = pltpu.roll(orig, HEAD_DIM, 1)
            klo_ref[cur, 2 * c, new, :] = jnp.where(lane_lo, orig, 0.0).astype(_BF16)
            khi_ref[cur, 2 * c, new, :] = jnp.where(lane_lo, 0.0, sw).astype(_BF16)
            klo_ref[cur, 2 * c + 1, new, :] = jnp.where(lane_lo, sw, 0.0).astype(_BF16)
            khi_ref[cur, 2 * c + 1, new, :] = jnp.where(lane_lo, 0.0, orig).astype(_BF16)
        vt_ref[cur, :, new] = kv[:, KV_W:2 * KV_W].T.astype(_BF16)
        for ref in (klo_ref, khi_ref):
            prev = jnp.zeros((N_KV_HEADS, BLOCK, LANES), _BF16) if first else ref[prv, :, TM:TM + BLOCK, :]
            ref[cur, :, 0:BLOCK, :] = prev
        vt_ref[cur, :, 0:BLOCK] = (jnp.zeros((KV_W, BLOCK), _BF16) if first
                                   else vt_ref[prv, :, TM:TM + BLOCK])

    def reduce_rows(x3, op):
        r = x3[0]
        for c in range(1, chunks):
            r = op(r, x3[c])
        for d in (1, 2, 4):
            r = op(r, pltpu.roll(r, d, 0))
        return r

    def softmax_band(s_full, sink_row, b, tri, prev_exists):
        s_prev = s_full[0:BLOCK]
        s_cur = s_full[BLOCK:2 * BLOCK]
        if b == 0:
            s_prev = jnp.where(prev_exists, s_prev, MASK_VALUE)
        s3 = jnp.where(tri, s_cur, s_prev).reshape(chunks, SUBLANES, stack)
        m = jnp.maximum(reduce_rows(s3, jnp.maximum), sink_row)
        e3 = jnp.exp(s3 - m[None])
        den = reduce_rows(e3, jnp.add) + jnp.exp(sink_row - m)
        pn = (e3 * (1.0 / den)[None]).reshape(BLOCK, stack)
        band = jnp.concatenate([jnp.where(tri, 0.0, pn), jnp.where(tri, pn, 0.0)], axis=0)
        return band.astype(_BF16)

    def sink_row(kvh, odd):
        cols = [jnp.full((SUBLANES, BLOCK), sink_ref[j, 2 * (PAIRS_PER_KV * kvh + i) + odd], _F32)
                for i in range(PAIRS_PER_KV)]
        return jnp.concatenate(cols, axis=1)

    def body(project, attend):
        if project:
            slot = _fetch_tile(step, in_copies)
            u_ref[...] = _rms(h_buf[slot], gpre_ref[layer:layer + 1, :]).astype(_BF16)
        if attend:
            kj = lax.broadcasted_iota(jnp.int32, (BLOCK, stack), 0)
            qi = lax.broadcasted_iota(jnp.int32, (BLOCK, stack), 1) & (BLOCK - 1)
            tri = kj <= qi
            prev_exists = (kj + ((step - 1) * TM - BLOCK)) >= 0

        for kvh in range(N_KV_HEADS):
            pairs = slice(PAIRS_PER_KV * kvh, PAIRS_PER_KV * (kvh + 1))
            if attend:
                scores = []
                for b in range(nb):
                    rows = slice(b * BLOCK, (b + 1) * BLOCK)
                    band = slice(b * BLOCK, (b + 2) * BLOCK)
                    qs = q3_ref[prv, pairs, rows, :].reshape(stack, LANES)
                    scores.append((_dot_nt(klo_ref[prv, kvh, band, :], qs),
                                   _dot_nt(khi_ref[prv, kvh, band, :], qs)))

            if project:
                q = _dot(u_ref[...], win_ref[:, kvh * per_kv:(kvh + 1) * per_kv])
                gate = _dot(u_ref[...],
                            win_ref[:, GATE_COL + kvh * per_kv:GATE_COL + (kvh + 1) * per_kv])
                for i in range(PAIRS_PER_KV):
                    p = PAIRS_PER_KV * kvh + i
                    q3_ref[cur, p] = (q[:, i * LANES:(i + 1) * LANES] * ATTN_SCALE).astype(_BF16)
                    g = gate[:, i * LANES:(i + 1) * LANES]
                    sg3_ref[cur, p] = g * jax.nn.sigmoid(g)

            if attend:
                sink_e = sink_row(kvh, 0)
                sink_o = sink_row(kvh, 1)
                for b in range(nb):
                    rows = slice(b * BLOCK, (b + 1) * BLOCK)
                    band = slice(b * BLOCK, (b + 2) * BLOCK)
                    vt = vt_ref[prv, kvh * HEAD_DIM:(kvh + 1) * HEAD_DIM, band]
                    o_e = _dot(vt, softmax_band(scores[b][0], sink_e, b, tri, prev_exists))
                    o_o = _dot(vt, softmax_band(scores[b][1], sink_o, b, tri, prev_exists))
                    o2 = jnp.concatenate([o_e, o_o], axis=0).T
                    sg = sg3_ref[prv, pairs, rows, :].reshape(stack, LANES)
                    y3_ref[pairs, rows, :] = (o2 * sg).astype(_BF16).reshape(PAIRS_PER_KV, BLOCK, LANES)

        if project:
            project_kv(first=not attend)

        if attend:
            y = jnp.concatenate([y3_ref[p] for p in range(N_PAIRS)], axis=1)
            yo = _dot(y, wout_ref[...])

            def store(oslot):
                o_buf[oslot] = h_buf[(step - 1) % H_SLOTS] + _rms(yo, gpost_ref[layer:layer + 1, :])
            _write_back(step, out_copies, store, last=not project)

    _staggered(step, body)


def _layer_spec(stacked, j):
    zeros = (0,) * (stacked.ndim - 1)
    return pl.BlockSpec((None,) + stacked.shape[1:], lambda i: (j,) + zeros,
                        pipeline_mode=pl.Buffered(1))


def _attn_layer(h, layer, j, norm_pre, norm_post, sinks, w_in, w_out, casts):
    band_rows = BLOCK + TM
    cast_in, cast_out, cast_shapes = _cast_specs(casts)
    return pl.pallas_call(
        functools.partial(_attn_kernel, layer, j, len(casts)),
        out_shape=[jax.ShapeDtypeStruct((SEQ, D_MODEL), _F32)] + cast_shapes,
        grid=(N_TILES + 1,),
        in_specs=[
            pl.BlockSpec(memory_space=pltpu.SMEM),
            pl.BlockSpec(memory_space=pl.ANY),
            _resident(norm_pre),
            _resident(norm_post),
            _resident(w_in),
            _resident(w_out),
        ] + cast_in,
        out_specs=[pl.BlockSpec(memory_space=pl.ANY)] + cast_out,
        scratch_shapes=[
            pltpu.VMEM((TM, D_MODEL), _BF16),
            pltpu.VMEM((2, N_PAIRS, TM, LANES), _BF16),
            pltpu.VMEM((2, N_PAIRS, TM, LANES), _F32),
            pltpu.VMEM((N_PAIRS, TM, LANES), _BF16),
            pltpu.VMEM((2, N_KV_HEADS, band_rows, LANES), _BF16),
            pltpu.VMEM((2, N_KV_HEADS, band_rows, LANES), _BF16),
            pltpu.VMEM((2, KV_W, band_rows), _BF16),
            pltpu.VMEM((H_SLOTS, TM, D_MODEL), _F32),
            pltpu.VMEM((O_SLOTS, TM, D_MODEL), _F32),
            pltpu.SemaphoreType.DMA((H_SLOTS,)),
            pltpu.SemaphoreType.DMA((O_SLOTS,)),
        ],
        compiler_params=pltpu.CompilerParams(
            dimension_semantics=("arbitrary",), vmem_limit_bytes=VMEM_LIMIT),
        name="attn_layer",
    )(sinks, h, norm_pre, norm_post, w_in, w_out, *[w for w, _ in casts])


SEG = TM // SUBLANES
CONV_TAIL = (CONV_W - 1) * SUBLANES


def _sublane_scan(a, b, row):
    for d in (1, 2, 4):
        keep = row >= d
        a_sh = jnp.where(keep, pltpu.roll(a, d, 0), 1.0)
        b_sh = jnp.where(keep, pltpu.roll(b, d, 0), 0.0)
        b = a * b_sh + b
        a = a * a_sh
    return a, b


def _lru_kernel(layer, j, n_cast, h_hbm, gpre_ref, gpost_ref, win_ref, cw_ref, cb_ref, wa_ref,
                ba_ref, wx_ref, bx_ref, lam_ref, wout_ref, *refs):
    cast_src, (out_hbm, *cast_dst), scratch = refs[:n_cast], refs[n_cast:2 * n_cast + 1], refs[2 * n_cast + 1:]
    (u_ref, x_ref, tail_ref, y_ref, yo_ref, sp_ref, hc_ref,
     h_buf, o_buf, in_sem, out_sem) = scratch
    _cast_rows(cast_src, cast_dst)
    step = pl.program_id(0)
    cur = step % 2
    prv = 1 - cur

    def in_copies(tile, slot):
        return [pltpu.make_async_copy(h_hbm.at[pl.ds(tile * TM + s * SEG, SEG), :],
                                      h_buf.at[slot, :, s, :], in_sem.at[slot])
                for s in range(SUBLANES)]

    def out_copies(tile, slot):
        return [pltpu.make_async_copy(o_buf.at[slot, :, s, :],
                                      out_hbm.at[pl.ds(tile * TM + s * SEG, SEG), :], out_sem.at[slot])
                for s in range(SUBLANES)]

    @pl.when(step == 0)
    def _():
        for c in in_copies(0, 0):
            c.start(priority=TILE_DMA_QUEUE)
        tail_ref[...] = jnp.zeros(tail_ref.shape, _F32)
        hc_ref[...] = jnp.zeros(hc_ref.shape, _F32)
        nl = -lam_ref[j:j + 1, :]
        softplus = jnp.maximum(nl, 0.0) + jnp.log1p(jnp.exp(-jnp.abs(nl)))
        sp_ref[...] = jnp.broadcast_to(C_RG * softplus, sp_ref.shape)

    def block_cols(blk):
        return slice(blk * LRU_BLOCK_W, (blk + 1) * LRU_BLOCK_W)

    def project_x(blk):
        cs = block_cols(blk)
        x_ref[:, cs] = _dot(u_ref[...], win_ref[:, cs])

    def out_project(blk):
        cs = block_cols(blk)
        yo_ref[:, cs] = _dot(y_ref[prv], wout_ref[:, cs])

    def recur_block(blk, row, first_segment):
        cs = block_cols(blk)
        gate = _dot(u_ref[...],
                    win_ref[:, D_MODEL + blk * LRU_BLOCK_W:D_MODEL + (blk + 1) * LRU_BLOCK_W])

        x = x_ref[:, cs]
        tail_cur = x[TM - CONV_TAIL:]
        tail_prev = tail_ref[:, cs]
        tail_ref[:, cs] = tail_cur
        wrap = []
        for i in range(CONV_W - 1):
            rows = slice(i * SUBLANES, (i + 1) * SUBLANES)
            wrap.append(jnp.where(first_segment, pltpu.roll(tail_prev[rows], 1, 0),
                                  pltpu.roll(tail_cur[rows], 1, 0)))
        xc = cb_ref[j:j + 1, cs] + cw_ref[CONV_W - 1:CONV_W, cs] * x
        for shift in range(1, CONV_W):
            tap = CONV_W - 1 - shift
            shifted = jnp.concatenate(wrap[CONV_W - 1 - shift:] + [x[:TM - shift * SUBLANES]], axis=0)
            xc = xc + cw_ref[tap:tap + 1, cs] * shifted
        xcb = xc.astype(_BF16)
        r = jax.nn.sigmoid(_dot(xcb, wa_ref[blk]) + ba_ref[j, blk:blk + 1, :])
        ig = jax.nn.sigmoid(_dot(xcb, wx_ref[blk]) + bx_ref[j, blk:blk + 1, :])
        sg = gate * jax.nn.sigmoid(gate)

        nla = r * sp_ref[0:1, cs]
        a = jnp.exp(-nla)
        z = jnp.tanh(nla) * (a * a + 1.0)
        mult = jnp.where(z > 0.0, z * lax.rsqrt(z), 0.0)
        b = mult * (ig * xc)

        acc_a = a[0:SUBLANES]
        acc_h = b[0:SUBLANES]
        prods, local = [acc_a], [acc_h]
        for g in range(1, SEG):
            rows = slice(g * SUBLANES, (g + 1) * SUBLANES)
            acc_h = a[rows] * acc_h + b[rows]
            acc_a = a[rows] * acc_a
            prods.append(acc_a)
            local.append(acc_h)
        carry_in = hc_ref[:, cs]
        seg_a, seg_h = _sublane_scan(acc_a, acc_h, row)
        seg_end = seg_h + seg_a * carry_in
        entering = jnp.where(first_segment, carry_in, pltpu.roll(seg_end, 1, 0))
        hc_ref[:, cs] = jnp.broadcast_to(seg_end[SUBLANES - 1:SUBLANES, :], (SUBLANES, LRU_BLOCK_W))
        ys = [(local[g] + prods[g] * entering) * sg[g * SUBLANES:(g + 1) * SUBLANES]
              for g in range(SEG)]
        y_ref[cur, :, cs] = jnp.concatenate(ys, axis=0).astype(_BF16)

    def body(recur, finish):
        if recur:
            slot = _fetch_tile(step, in_copies)
        if finish:
            out_project(0)
            out_project(1)
        if recur:
            row = lax.broadcasted_iota(jnp.int32, (SUBLANES, LRU_BLOCK_W), 0)
            first_segment = row == 0
            u_ref[...] = _rms(h_buf[slot].reshape(TM, D_MODEL),
                              gpre_ref[layer:layer + 1, :]).astype(_BF16)
            project_x(0)
        for blk in range(N_LRU_BLOCKS):
            if recur and blk + 1 < N_LRU_BLOCKS:
                project_x(blk + 1)
            if finish and blk + 2 < N_LRU_BLOCKS:
                out_project(blk + 2)
            if recur:
                recur_block(blk, row, first_segment)
        if finish:
            def store(oslot):
                h_prev = h_buf[(step - 1) % H_SLOTS].reshape(TM, D_MODEL)
                out = h_prev + _rms(yo_ref[...], gpost_ref[layer:layer + 1, :])
                o_buf[oslot] = out.reshape(SEG, SUBLANES, D_MODEL)
            _write_back(step, out_copies, store, last=not recur)

    _staggered(step, body)


def _lru_layer(h, layer, j, norm_pre, norm_post, small, weights, casts):
    conv_w, conv_b, b_a, b_x, lam = small
    w_in, w_a, w_x, w_out = weights
    gate_shape = (N_LRU_BLOCKS, LRU_BLOCK_W, LRU_BLOCK_W)
    w_a, w_x = w_a.reshape(gate_shape), w_x.reshape(gate_shape)
    cast_in, cast_out, cast_shapes = _cast_specs(casts)
    return pl.pallas_call(
        functools.partial(_lru_kernel, layer, j, len(casts)),
        out_shape=[jax.ShapeDtypeStruct((SEQ, D_MODEL), _F32)] + cast_shapes,
        grid=(N_TILES + 1,),
        in_specs=[
            pl.BlockSpec(memory_space=pl.ANY),
            _resident(norm_pre),
            _resident(norm_post),
            _resident(w_in), _layer_spec(conv_w, j), _resident(conv_b),
            _resident(w_a), _resident(b_a), _resident(w_x), _resident(b_x),
            _resident(lam), _resident(w_out),
        ] + cast_in,
        out_specs=[pl.BlockSpec(memory_space=pl.ANY)] + cast_out,
        scratch_shapes=[
            pltpu.VMEM((TM, D_MODEL), _BF16),
            pltpu.VMEM((TM, D_MODEL), _F32),
            pltpu.VMEM((CONV_TAIL, D_MODEL), _F32),
            pltpu.VMEM((2, TM, D_MODEL), _BF16),
            pltpu.VMEM((TM, D_MODEL), _F32),
            pltpu.VMEM((SUBLANES, D_MODEL), _F32),
            pltpu.VMEM((SUBLANES, D_MODEL), _F32),
            pltpu.VMEM((H_SLOTS, SEG, SUBLANES, D_MODEL), _F32),
            pltpu.VMEM((O_SLOTS, SEG, SUBLANES, D_MODEL), _F32),
            pltpu.SemaphoreType.DMA((H_SLOTS,)),
            pltpu.SemaphoreType.DMA((O_SLOTS,)),
        ],
        compiler_params=pltpu.CompilerParams(
            dimension_semantics=("arbitrary",), vmem_limit_bytes=VMEM_LIMIT),
        name="lru_layer",
    )(h, norm_pre, norm_post, w_in, conv_w, conv_b, w_a, b_a, w_x, b_x, lam, w_out,
      *[w for w, _ in casts])


def kernel(x, norm_pre, norm_post, attn_w_in, attn_w_out, attn_sinks, lru_w_in, lru_conv_w,
           lru_conv_b, lru_w_a, lru_b_a, lru_w_x, lru_b_x, lru_lambda, lru_w_out):
    depth = norm_pre.shape[0]
    lru_small = (lru_conv_w, lru_conv_b, lru_b_a, lru_b_x, lru_lambda)
    rows = lambda w: w.reshape(w.shape[0], D_MODEL, -1)
    attn_f32 = (attn_w_in, attn_w_out)
    lru_f32 = (lru_w_in, rows(lru_w_a), rows(lru_w_x), lru_w_out)
    weights = [w[0].astype(_BF16) for w in attn_f32]
    h = x.reshape(SEQ, D_MODEL)
    for layer in range(depth):
        j = layer // 2
        last = layer == depth - 1
        if layer % 2 == 0:
            casts = [] if last else [(w, j) for w in lru_f32]
            h, *weights = _attn_layer(h, layer, j, norm_pre, norm_post, attn_sinks, *weights, casts)
        else:
            casts = [] if last else [(w, j + 1) for w in attn_f32]
            h, *weights = _lru_layer(h, layer, j, norm_pre, norm_post, lru_small, weights, casts)
    return h.reshape(x.shape)
```

```python
import functools

import jax
import jax.numpy as jnp
from jax import lax
from jax.experimental import pallas as pl
from jax.experimental.pallas import tpu as pltpu

D_MODEL = 2048
SEQ = 8192
HEAD_DIM = 64
N_HEADS = 32
N_KV_HEADS = 4
GROUP = N_HEADS // N_KV_HEADS
BLOCK = 128
KV_W = N_KV_HEADS * HEAD_DIM
GATE_COL = D_MODEL + 2 * KV_W
LRU_BLOCK_W = 256
N_LRU_BLOCKS = D_MODEL // LRU_BLOCK_W
CONV_W = 4
C_RG = 8.0
NORM_EPS = 1e-6
MASK_VALUE = -1e30
ATTN_SCALE = HEAD_DIM ** -0.5

LANES = 128
SUBLANES = 8
N_PAIRS = N_HEADS // 2
PAIRS_PER_KV = GROUP // 2
SOFTMAX_PAIRS = 2

TM = 256
N_TILES = SEQ // TM
VMEM_LIMIT = 60 * 1024 * 1024

_F32 = jnp.float32
_BF16 = jnp.bfloat16


def _rms(x, g):
    ms = jnp.mean(x * x, axis=-1, keepdims=True)
    return x * lax.rsqrt(ms + NORM_EPS) * g


def _dot(a, b):
    return jnp.dot(a, b, preferred_element_type=_F32)


def _dot_nt(a, b):
    return lax.dot_general(a, b, (((1,), (1,)), ((), ())), preferred_element_type=_F32)


CAST_ROWS = D_MODEL // N_TILES


def _cast_rows(srcs, dsts):
    for src, dst in zip(srcs, dsts):
        dst[...] = src[...].astype(_BF16)


def _cast_specs(casts):
    rows = lambda i: jnp.minimum(i, N_TILES - 1)
    in_specs = [pl.BlockSpec((None, CAST_ROWS, w.shape[-1]), lambda i, j=j: (j, rows(i), 0))
                for w, j in casts]
    out_specs = [pl.BlockSpec((CAST_ROWS, w.shape[-1]), lambda i: (rows(i), 0)) for w, _ in casts]
    out_shapes = [jax.ShapeDtypeStruct((D_MODEL, w.shape[-1]), _BF16) for w, _ in casts]
    return in_specs, out_specs, out_shapes


def _resident(arr):
    zeros = (0,) * arr.ndim
    return pl.BlockSpec(arr.shape, lambda i: zeros, pipeline_mode=pl.Buffered(1))


H_SLOTS = 3
O_SLOTS = 2


def _fetch_tile(step, in_copies):
    slot = step % H_SLOTS
    for c in in_copies(step, slot):
        c.wait()

    @pl.when(step + 1 < N_TILES)
    def _():
        for c in in_copies(step + 1, (step + 1) % H_SLOTS):
            c.start()
    return slot


def _write_back(step, out_copies, store, last):
    slot = (step - 1) % O_SLOTS

    @pl.when(step >= 1 + O_SLOTS)
    def _():
        for c in out_copies(step - 1 - O_SLOTS, slot):
            c.wait()
    store(slot)
    for c in out_copies(step - 1, slot):
        c.start()
    if last:
        for c in out_copies(step - 2, 1 - slot) + out_copies(step - 1, slot):
            c.wait()


def _staggered(step, body):
    pl.when(step == 0)(lambda: body(True, False))
    pl.when(jnp.logical_and(step > 0, step < N_TILES))(lambda: body(True, True))
    pl.when(step == N_TILES)(lambda: body(False, True))


def _attn_kernel(layer, j, n_cast, sink_ref, h_hbm, gpre_ref, gpost_ref, win_ref, wout_ref, *refs):
    cast_src, (out_hbm, *cast_dst), scratch = refs[:n_cast], refs[n_cast:2 * n_cast + 1], refs[2 * n_cast + 1:]
    (u_ref, q3_ref, sg3_ref, y3_ref, klo_ref, khi_ref, vt_ref,
     h_buf, o_buf, in_sem, out_sem) = scratch
    _cast_rows(cast_src, cast_dst)
    step = pl.program_id(0)
    cur = step % 2
    prv = 1 - cur

    def in_copies(tile, slot):
        return [pltpu.make_async_copy(h_hbm.at[pl.ds(tile * TM, TM), :], h_buf.at[slot],
                                      in_sem.at[slot])]

    def out_copies(tile, slot):
        return [pltpu.make_async_copy(o_buf.at[slot], out_hbm.at[pl.ds(tile * TM, TM), :],
                                      out_sem.at[slot])]

    @pl.when(step == 0)
    def _():
        for c in in_copies(0, 0):
            c.start()

    nb = TM // BLOCK
    stack = SOFTMAX_PAIRS * BLOCK
    chunks = BLOCK // SUBLANES
    per_kv = GROUP * HEAD_DIM

    def project_kv(first):
        kv = _dot(u_ref[...], win_ref[:, D_MODEL:GATE_COL])
        lane_lo = lax.broadcasted_iota(jnp.int32, (TM, LANES), 1) < HEAD_DIM
        new = slice(BLOCK, BLOCK + TM)
        for c in range(N_KV_HEADS // 2):
            orig = kv[:, c * LANES:(c + 1) * LANES]
            sw = pltpu.roll(orig, HEAD_DIM, 1)
            klo_ref[cur, 2 * c, new, :] = jnp.where(lane_lo, orig, 0.0).astype(_BF16)
            khi_ref[cur, 2 * c, new, :] = jnp.where(lane_lo, 0.0, sw).astype(_BF16)
            klo_ref[cur, 2 * c + 1, new, :] = jnp.where(lane_lo, sw, 0.0).astype(_BF16)
            khi_ref[cur, 2 * c + 1, new, :] = jnp.where(lane_lo, 0.0, orig).astype(_BF16)
        vt_ref[cur, :, new] = kv[:, KV_W:2 * KV_W].T.astype(_BF16)
        for ref in (klo_ref, khi_ref):
            prev = jnp.zeros((N_KV_HEADS, BLOCK, LANES), _BF16) if first else ref[prv, :, TM:TM + BLOCK, :]
            ref[cur, :, 0:BLOCK, :] = prev
        vt_ref[cur, :, 0:BLOCK] = (jnp.zeros((KV_W, BLOCK), _BF16) if first
                                   else vt_ref[prv, :, TM:TM + BLOCK])

    def reduce_rows(x3, op):
        r = x3[0]
        for c in range(1, chunks):
            r = op(r, x3[c])
        for d in (1, 2, 4):
            r = op(r, pltpu.roll(r, d, 0))
        return r

    def softmax_band(s_full, sink_row, b, tri, prev_exists):
        s_prev = s_full[0:BLOCK]
        s_cur = s_full[BLOCK:2 * BLOCK]
        if b == 0:
            s_prev = jnp.where(prev_exists, s_prev, MASK_VALUE)
        s3 = jnp.where(tri, s_cur, s_prev).reshape(chunks, SUBLANES, stack)
        m = jnp.maximum(reduce_rows(s3, jnp.maximum), sink_row)
        e3 = jnp.exp(s3 - m[None])
        den = reduce_rows(e3, jnp.add) + jnp.exp(sink_row - m)
        pn = (e3 * (1.0 / den)[None]).reshape(BLOCK, stack)
        band = jnp.concatenate([jnp.where(tri, 0.0, pn), jnp.where(tri, pn, 0.0)], axis=0)
        return band.astype(_BF16)

    def sink_row(first_pair, odd):
        cols = [jnp.full((SUBLANES, BLOCK), sink_ref[j, 2 * (first_pair + i) + odd], _F32)
                for i in range(SOFTMAX_PAIRS)]
        return jnp.concatenate(cols, axis=1)

    def body(project, attend):
        if project:
            slot = _fetch_tile(step, in_copies)
            u_ref[...] = _rms(h_buf[slot], gpre_ref[layer:layer + 1, :]).astype(_BF16)
        if attend:
            kj = lax.broadcasted_iota(jnp.int32, (BLOCK, stack), 0)
            qi = lax.broadcasted_iota(jnp.int32, (BLOCK, stack), 1) & (BLOCK - 1)
            tri = kj <= qi
            prev_exists = (kj + ((step - 1) * TM - BLOCK)) >= 0

        groups = [slice(first, first + SOFTMAX_PAIRS)
                  for first in range(0, PAIRS_PER_KV, SOFTMAX_PAIRS)]
        for kvh in range(N_KV_HEADS):
            base = PAIRS_PER_KV * kvh
            if attend:
                scores = {}
                for g, grp in enumerate(groups):
                    for b in range(nb):
                        rows = slice(b * BLOCK, (b + 1) * BLOCK)
                        band = slice(b * BLOCK, (b + 2) * BLOCK)
                        qs = q3_ref[prv, base + grp.start:base + grp.stop, rows, :].reshape(stack, LANES)
                        scores[g, b] = (_dot_nt(klo_ref[prv, kvh, band, :], qs),
                                        _dot_nt(khi_ref[prv, kvh, band, :], qs))

            if project:
                q = _dot(u_ref[...], win_ref[:, kvh * per_kv:(kvh + 1) * per_kv])
                gate = _dot(u_ref[...],
                            win_ref[:, GATE_COL + kvh * per_kv:GATE_COL + (kvh + 1) * per_kv])
                for i in range(PAIRS_PER_KV):
                    p = PAIRS_PER_KV * kvh + i
                    q3_ref[cur, p] = (q[:, i * LANES:(i + 1) * LANES] * ATTN_SCALE).astype(_BF16)
                    g = gate[:, i * LANES:(i + 1) * LANES]
                    sg3_ref[cur, p] = g * jax.nn.sigmoid(g)

            if attend:
                for g, grp in enumerate(groups):
                    pairs = slice(base + grp.start, base + grp.stop)
                    sink_e = sink_row(pairs.start, 0)
                    sink_o = sink_row(pairs.start, 1)
                    for b in range(nb):
                        rows = slice(b * BLOCK, (b + 1) * BLOCK)
                        band = slice(b * BLOCK, (b + 2) * BLOCK)
                        vt = vt_ref[prv, kvh * HEAD_DIM:(kvh + 1) * HEAD_DIM, band]
                        o_e = _dot(vt, softmax_band(scores[g, b][0], sink_e, b, tri, prev_exists))
                        o_o = _dot(vt, softmax_band(scores[g, b][1], sink_o, b, tri, prev_exists))
                        o2 = jnp.concatenate([o_e, o_o], axis=0).T
                        sg = sg3_ref[prv, pairs, rows, :].reshape(stack, LANES)
                        y3_ref[pairs, rows, :] = (o2 * sg).astype(_BF16).reshape(SOFTMAX_PAIRS, BLOCK, LANES)

        if project:
            project_kv(first=not attend)

        if attend:
            y = jnp.concatenate([y3_ref[p] for p in range(N_PAIRS)], axis=1)
            yo = _dot(y, wout_ref[...])

            def store(oslot):
                o_buf[oslot] = h_buf[(step - 1) % H_SLOTS] + _rms(yo, gpost_ref[layer:layer + 1, :])
            _write_back(step, out_copies, store, last=not project)

    _staggered(step, body)


def _layer_spec(stacked, j):
    zeros = (0,) * (stacked.ndim - 1)
    return pl.BlockSpec((None,) + stacked.shape[1:], lambda i: (j,) + zeros,
                        pipeline_mode=pl.Buffered(1))


def _attn_layer(h, layer, j, norm_pre, norm_post, sinks, w_in, w_out, casts):
    band_rows = BLOCK + TM
    cast_in, cast_out, cast_shapes = _cast_specs(casts)
    return pl.pallas_call(
        functools.partial(_attn_kernel, layer, j, len(casts)),
        out_shape=[jax.ShapeDtypeStruct((SEQ, D_MODEL), _F32)] + cast_shapes,
        grid=(N_TILES + 1,),
        in_specs=[
            pl.BlockSpec(memory_space=pltpu.SMEM),
            pl.BlockSpec(memory_space=pl.ANY),
            _resident(norm_pre),
            _resident(norm_post),
            _resident(w_in),
            _resident(w_out),
        ] + cast_in,
        out_specs=[pl.BlockSpec(memory_space=pl.ANY)] + cast_out,
        scratch_shapes=[
            pltpu.VMEM((TM, D_MODEL), _BF16),
            pltpu.VMEM((2, N_PAIRS, TM, LANES), _BF16),
            pltpu.VMEM((2, N_PAIRS, TM, LANES), _F32),
            pltpu.VMEM((N_PAIRS, TM, LANES), _BF16),
            pltpu.VMEM((2, N_KV_HEADS, band_rows, LANES), _BF16),
            pltpu.VMEM((2, N_KV_HEADS, band_rows, LANES), _BF16),
            pltpu.VMEM((2, KV_W, band_rows), _BF16),
            pltpu.VMEM((H_SLOTS, TM, D_MODEL), _F32),
            pltpu.VMEM((O_SLOTS, TM, D_MODEL), _F32),
            pltpu.SemaphoreType.DMA((H_SLOTS,)),
            pltpu.SemaphoreType.DMA((O_SLOTS,)),
        ],
        compiler_params=pltpu.CompilerParams(
            dimension_semantics=("arbitrary",), vmem_limit_bytes=VMEM_LIMIT),
        name="attn_layer",
    )(sinks, h, norm_pre, norm_post, w_in, w_out, *[w for w, _ in casts])


SEG = TM // SUBLANES
CONV_TAIL = (CONV_W - 1) * SUBLANES


def _sublane_scan(a, b, row):
    for d in (1, 2, 4):
        keep = row >= d
        a_sh = jnp.where(keep, pltpu.roll(a, d, 0), 1.0)
        b_sh = jnp.where(keep, pltpu.roll(b, d, 0), 0.0)
        b = a * b_sh + b
        a = a * a_sh
    return a, b


def _lru_kernel(layer, j, n_cast, h_hbm, gpre_ref, gpost_ref, win_ref, cw_ref, cb_ref, wa_ref,
                ba_ref, wx_ref, bx_ref, lam_ref, wout_ref, *refs):
    cast_src, (out_hbm, *cast_dst), scratch = refs[:n_cast], refs[n_cast:2 * n_cast + 1], refs[2 * n_cast + 1:]
    (u_ref, x_ref, tail_ref, y_ref, yo_ref, sp_ref, hc_ref,
     h_buf, o_buf, in_sem, out_sem) = scratch
    _cast_rows(cast_src, cast_dst)
    step = pl.program_id(0)
    cur = step % 2
    prv = 1 - cur

    def in_copies(tile, slot):
        return [pltpu.make_async_copy(h_hbm.at[pl.ds(tile * TM + s * SEG, SEG), :],
                                      h_buf.at[slot, :, s, :], in_sem.at[slot])
                for s in range(SUBLANES)]

    def out_copies(tile, slot):
        return [pltpu.make_async_copy(o_buf.at[slot, :, s, :],
                                      out_hbm.at[pl.ds(tile * TM + s * SEG, SEG), :], out_sem.at[slot])
                for s in range(SUBLANES)]

    @pl.when(step == 0)
    def _():
        for c in in_copies(0, 0):
            c.start()
        tail_ref[...] = jnp.zeros(tail_ref.shape, _F32)
        hc_ref[...] = jnp.zeros(hc_ref.shape, _F32)
        nl = -lam_ref[j:j + 1, :]
        softplus = jnp.maximum(nl, 0.0) + jnp.log1p(jnp.exp(-jnp.abs(nl)))
        sp_ref[...] = jnp.broadcast_to(C_RG * softplus, sp_ref.shape)

    def block_cols(blk):
        return slice(blk * LRU_BLOCK_W, (blk + 1) * LRU_BLOCK_W)

    def project_x(blk):
        cs = block_cols(blk)
        x_ref[:, cs] = _dot(u_ref[...], win_ref[:, cs])

    def out_project(blk):
        cs = block_cols(blk)
        yo_ref[:, cs] = _dot(y_ref[prv], wout_ref[:, cs])

    def recur_block(blk, row, first_segment):
        cs = block_cols(blk)
        gate = _dot(u_ref[...],
                    win_ref[:, D_MODEL + blk * LRU_BLOCK_W:D_MODEL + (blk + 1) * LRU_BLOCK_W])

        x = x_ref[:, cs]
        tail_cur = x[TM - CONV_TAIL:]
        tail_prev = tail_ref[:, cs]
        tail_ref[:, cs] = tail_cur
        wrap = []
        for i in range(CONV_W - 1):
            rows = slice(i * SUBLANES, (i + 1) * SUBLANES)
            wrap.append(jnp.where(first_segment, pltpu.roll(tail_prev[rows], 1, 0),
                                  pltpu.roll(tail_cur[rows], 1, 0)))
        xc = cb_ref[j:j + 1, cs] + cw_ref[CONV_W - 1:CONV_W, cs] * x
        for shift in range(1, CONV_W):
            tap = CONV_W - 1 - shift
            shifted = jnp.concatenate(wrap[CONV_W - 1 - shift:] + [x[:TM - shift * SUBLANES]], axis=0)
            xc = xc + cw_ref[tap:tap + 1, cs] * shifted
        xcb = xc.astype(_BF16)
        r = jax.nn.sigmoid(_dot(xcb, wa_ref[blk]) + ba_ref[j, blk:blk + 1, :])
        ig = jax.nn.sigmoid(_dot(xcb, wx_ref[blk]) + bx_ref[j, blk:blk + 1, :])
        sg = gate * jax.nn.sigmoid(gate)

        nla = r * sp_ref[0:1, cs]
        a = jnp.exp(-nla)
        z = jnp.tanh(nla) * (a * a + 1.0)
        mult = jnp.where(z > 0.0, z * lax.rsqrt(z), 0.0)
        b = mult * (ig * xc)

        acc_a = a[0:SUBLANES]
        acc_h = b[0:SUBLANES]
        prods, local = [acc_a], [acc_h]
        for g in range(1, SEG):
            rows = slice(g * SUBLANES, (g + 1) * SUBLANES)
            acc_h = a[rows] * acc_h + b[rows]
            acc_a = a[rows] * acc_a
            prods.append(acc_a)
            local.append(acc_h)
        carry_in = hc_ref[:, cs]
        seg_a, seg_h = _sublane_scan(acc_a, acc_h, row)
        seg_end = seg_h + seg_a * carry_in
        entering = jnp.where(first_segment, carry_in, pltpu.roll(seg_end, 1, 0))
        hc_ref[:, cs] = jnp.broadcast_to(seg_end[SUBLANES - 1:SUBLANES, :], (SUBLANES, LRU_BLOCK_W))
        ys = [(local[g] + prods[g] * entering) * sg[g * SUBLANES:(g + 1) * SUBLANES]
              for g in range(SEG)]
        y_ref[cur, :, cs] = jnp.concatenate(ys, axis=0).astype(_BF16)

    def body(recur, finish):
        if recur:
            slot = _fetch_tile(step, in_copies)
        if finish:
            out_project(0)
            out_project(1)
        if recur:
            row = lax.broadcasted_iota(jnp.int32, (SUBLANES, LRU_BLOCK_W), 0)
            first_segment = row == 0
            u_ref[...] = _rms(h_buf[slot].reshape(TM, D_MODEL),
                              gpre_ref[layer:layer + 1, :]).astype(_BF16)
            project_x(0)
        for blk in range(N_LRU_BLOCKS):
            if recur and blk + 1 < N_LRU_BLOCKS:
                project_x(blk + 1)
            if finish and blk + 2 < N_LRU_BLOCKS:
                out_project(blk + 2)
            if recur:
                recur_block(blk, row, first_segment)
        if finish:
            def store(oslot):
                h_prev = h_buf[(step - 1) % H_SLOTS].reshape(TM, D_MODEL)
                out = h_prev + _rms(yo_ref[...], gpost_ref[layer:layer + 1, :])
                o_buf[oslot] = out.reshape(SEG, SUBLANES, D_MODEL)
            _write_back(step, out_copies, store, last=not recur)

    _staggered(step, body)


def _lru_layer(h, layer, j, norm_pre, norm_post, small, weights, casts):
    conv_w, conv_b, b_a, b_x, lam = small
    w_in, w_a, w_x, w_out = weights
    gate_shape = (N_LRU_BLOCKS, LRU_BLOCK_W, LRU_BLOCK_W)
    w_a, w_x = w_a.reshape(gate_shape), w_x.reshape(gate_shape)
    cast_in, cast_out, cast_shapes = _cast_specs(casts)
    return pl.pallas_call(
        functools.partial(_lru_kernel, layer, j, len(casts)),
        out_shape=[jax.ShapeDtypeStruct((SEQ, D_MODEL), _F32)] + cast_shapes,
        grid=(N_TILES + 1,),
        in_specs=[
            pl.BlockSpec(memory_space=pl.ANY),
            _resident(norm_pre),
            _resident(norm_post),
            _resident(w_in), _layer_spec(conv_w, j), _resident(conv_b),
            _resident(w_a), _resident(b_a), _resident(w_x), _resident(b_x),
            _resident(lam), _resident(w_out),
        ] + cast_in,
        out_specs=[pl.BlockSpec(memory_space=pl.ANY)] + cast_out,
        scratch_shapes=[
            pltpu.VMEM((TM, D_MODEL), _BF16),
            pltpu.VMEM((TM, D_MODEL), _F32),
            pltpu.VMEM((CONV_TAIL, D_MODEL), _F32),
            pltpu.VMEM((2, TM, D_MODEL), _BF16),
            pltpu.VMEM((TM, D_MODEL), _F32),
            pltpu.VMEM((SUBLANES, D_MODEL), _F32),
            pltpu.VMEM((SUBLANES, D_MODEL), _F32),
            pltpu.VMEM((H_SLOTS, SEG, SUBLANES, D_MODEL), _F32),
            pltpu.VMEM((O_SLOTS, SEG, SUBLANES, D_MODEL), _F32),
            pltpu.SemaphoreType.DMA((H_SLOTS,)),
            pltpu.SemaphoreType.DMA((O_SLOTS,)),
        ],
        compiler_params=pltpu.CompilerParams(
            dimension_semantics=("arbitrary",), vmem_limit_bytes=VMEM_LIMIT),
        name="lru_layer",
    )(h, norm_pre, norm_post, w_in, conv_w, conv_b, w_a, b_a, w_x, b_x, lam, w_out,
      *[w for w, _ in casts])


def kernel(x, norm_pre, norm_post, attn_w_in, attn_w_out, attn_sinks, lru_w_in, lru_conv_w,
           lru_conv_b, lru_w_a, lru_b_a, lru_w_x, lru_b_x, lru_lambda, lru_w_out):
    depth = norm_pre.shape[0]
    lru_small = (lru_conv_w, lru_conv_b, lru_b_a, lru_b_x, lru_lambda)
    rows = lambda w: w.reshape(w.shape[0], D_MODEL, -1)
    attn_f32 = (attn_w_in, attn_w_out)
    lru_f32 = (lru_w_in, rows(lru_w_a), rows(lru_w_x), lru_w_out)
    weights = [w[0].astype(_BF16) for w in attn_f32]
    h = x.reshape(SEQ, D_MODEL)
    for layer in range(depth):
        j = layer // 2
        last = layer == depth - 1
        if layer % 2 == 0:
            casts = [] if last else [(w, j) for w in lru_f32]
            h, *weights = _attn_layer(h, layer, j, norm_pre, norm_post, attn_sinks, *weights, casts)
        else:
            casts = [] if last else [(w, j + 1) for w in attn_f32]
            h, *weights = _lru_layer(h, layer, j, norm_pre, norm_post, lru_small, weights, casts)
    return h.reshape(x.shape)
```
